```python
import jax
import jax.numpy as jnp
from jax import lax
import numpy as np


D_MODEL = 2048
BATCH = 1
SEQ = 16384
DEPTH = 2

GRID_W = 64
CTX_LEN = 256
N_MIXERS = 2
EPS = 1e-6
ROPE_THETA = 10000.0
Q_BLOCK = 128

MLA_HEADS = 16
Q_LORA = 512
KV_LORA = 512
QK_NOPE = 128
QK_ROPE = 64
QK_HEAD = QK_NOPE + QK_ROPE
V_DIM = 128

NA_HEADS = 16
NA_HEAD_DIM = D_MODEL // NA_HEADS
NA_KH = 8
NA_KW = 16

N_EXPERTS = 16
EXPERT_FF = 1408
EC_FACTOR = 2

kernel_name = 'hybrid_mla_natten_ecmoe_dit'


def rmsnorm(x, g):
    xf = x.astype(jnp.float32)
    y = xf * lax.rsqrt(jnp.mean(xf * xf, axis=-1, keepdims=True) + EPS)
    return (y * g.astype(jnp.float32)).astype(x.dtype)


def modulate(h, shift, scale):
    return h * (1.0 + scale) + shift


def axial_angles(n_tokens, rot_dim):
    half = rot_dim // 2
    inv = ROPE_THETA ** (-jnp.arange(0, half, 2, dtype=jnp.float32) / half)
    t = jnp.arange(n_tokens)
    row = (t // GRID_W).astype(jnp.float32)
    col = (t % GRID_W).astype(jnp.float32)
    return row[:, None] * inv[None, :], col[:, None] * inv[None, :]


def rotate(x, ang):
    m = ang.shape[-1]
    cos = jnp.cos(ang)[:, None, :].astype(x.dtype)
    sin = jnp.sin(ang)[:, None, :].astype(x.dtype)
    x1, x2 = x[..., :m], x[..., m:]
    return jnp.concatenate([x1 * cos - x2 * sin, x2 * cos + x1 * sin], axis=-1)


def apply_axial_rope(x, ang_r, ang_c):
    half = x.shape[-1] // 2
    return jnp.concatenate([rotate(x[..., :half], ang_r), rotate(x[..., half:], ang_c)], axis=-1)


def softmax_attend(q, k, v, scale):
    s = jnp.einsum('bqhd,bkhd->bhqk', q, k).astype(jnp.float32) * scale
    p = jax.nn.softmax(s, axis=-1).astype(v.dtype)
    return jnp.einsum('bhqk,bkhd->bqhd', p, v)


def mla_mixer(h_lat, h_ctx, w_in, q_a_gain, kv_a_gain, w_qb, w_kvb, q_gain, k_gain, w_o,
              ang_r, ang_c, with_ctx):
    def project(h, use_rope):
        B, T, _ = h.shape
        a = h @ w_in
        q_c = a[..., :Q_LORA]
        kv_c = a[..., Q_LORA:Q_LORA + KV_LORA]
        k_pe = a[..., Q_LORA + KV_LORA:]
        q = (rmsnorm(q_c, q_a_gain) @ w_qb).reshape(B, T, MLA_HEADS, QK_HEAD)
        kv = (rmsnorm(kv_c, kv_a_gain) @ w_kvb).reshape(B, T, MLA_HEADS, QK_NOPE + V_DIM)
        k = jnp.concatenate(
            [kv[..., :QK_NOPE], jnp.broadcast_to(k_pe[:, :, None, :], (B, T, MLA_HEADS, QK_ROPE))],
            axis=-1)
        v = kv[..., QK_NOPE:]
        q = rmsnorm(q, q_gain)
        k = rmsnorm(k, k_gain)
        if use_rope:
            q = jnp.concatenate([q[..., :QK_NOPE], apply_axial_rope(q[..., QK_NOPE:], ang_r, ang_c)], -1)
            k = jnp.concatenate([k[..., :QK_NOPE], apply_axial_rope(k[..., QK_NOPE:], ang_r, ang_c)], -1)
        return q, k, v

    B, N, _ = h_lat.shape
    scale = QK_HEAD ** -0.5
    q_l, k_l, v_l = project(h_lat, True)
    q_c, k_c, v_c = project(h_ctx, False)
    k_all = jnp.concatenate([k_c, k_l], axis=1)
    v_all = jnp.concatenate([v_c, v_l], axis=1)
    nb = N // Q_BLOCK
    qb = jnp.moveaxis(q_l.reshape(B, nb, Q_BLOCK, MLA_HEADS, QK_HEAD), 1, 0)
    o = lax.map(lambda qq: softmax_attend(qq, k_all, v_all, scale), qb)
    o_lat = jnp.moveaxis(o, 0, 1).reshape(B, N, MLA_HEADS * V_DIM) @ w_o
    o_ctx = None
    if with_ctx:
        L = h_ctx.shape[1]
        o_ctx = softmax_attend(q_c, k_c, v_c, scale).reshape(B, L, MLA_HEADS * V_DIM) @ w_o
    return o_lat, o_ctx


def na_mixer(h_lat, h_ctx, w_qkv, q_gain, k_gain, rpb, w_o, with_ctx):
    def project(h):
        B, T, _ = h.shape
        qkv = (h @ w_qkv).reshape(B, T, 3, NA_HEADS, NA_HEAD_DIM)
        return rmsnorm(qkv[:, :, 0], q_gain), rmsnorm(qkv[:, :, 1], k_gain), qkv[:, :, 2]

    B, N, _ = h_lat.shape
    rows = N // GRID_W
    kh = min(NA_KH, rows)
    scale = NA_HEAD_DIM ** -0.5
    q_l, k_l, v_l = project(h_lat)
    q_c, k_c, v_c = project(h_ctx)
    qg = q_l.reshape(B, rows, GRID_W, NA_HEADS, NA_HEAD_DIM)
    kg = k_l.reshape(B, rows, GRID_W, NA_HEADS, NA_HEAD_DIM)
    vg = v_l.reshape(B, rows, GRID_W, NA_HEADS, NA_HEAD_DIM)

    col = jnp.arange(GRID_W)
    col_start = jnp.clip(col - NA_KW // 2, 0, GRID_W - NA_KW)
    col_mask = (col[None, :] >= col_start[:, None]) & (col[None, :] < col_start[:, None] + NA_KW)
    dc_idx = jnp.clip(col[None, :] - col[:, None], -(NA_KW - 1), NA_KW - 1) + NA_KW - 1
    row_start = jnp.clip(jnp.arange(rows) - kh // 2, 0, rows - kh)

    def row_fn(r):
        r0 = row_start[r]
        k_win = lax.dynamic_slice_in_dim(kg, r0, kh, axis=1)
        v_win = lax.dynamic_slice_in_dim(vg, r0, kh, axis=1)
        q_r = lax.dynamic_index_in_dim(qg, r, axis=1, keepdims=False)
        dr = r0 + jnp.arange(kh) - r
        bias = rpb[:, dr + NA_KH - 1][:, :, dc_idx].transpose(0, 2, 1, 3)
        s_win = (jnp.einsum('bqhd,bjkhd->bhqjk', q_r, k_win).astype(jnp.float32) * scale
                 + bias[None].astype(jnp.float32))
        s_win = jnp.where(col_mask[None, None, :, None, :], s_win, -jnp.inf)
        s_ctx = jnp.einsum('bqhd,bkhd->bhqk', q_r, k_c).astype(jnp.float32) * scale
        s = jnp.concatenate([s_win.reshape(B, NA_HEADS, GRID_W, kh * GRID_W), s_ctx], axis=-1)
        p = jax.nn.softmax(s, axis=-1).astype(v_l.dtype)
        p_win = p[..., :kh * GRID_W].reshape(B, NA_HEADS, GRID_W, kh, GRID_W)
        p_ctx = p[..., kh * GRID_W:]
        return (jnp.einsum('bhqjk,bjkhd->bqhd', p_win, v_win)
                + jnp.einsum('bhqk,bkhd->bqhd', p_ctx, v_c))

    o = lax.map(row_fn, jnp.arange(rows))
    o_lat = jnp.moveaxis(o, 0, 1).reshape(B, N, D_MODEL) @ w_o
    o_ctx = None
    if with_ctx:
        L = h_ctx.shape[1]
        o_ctx = softmax_attend(q_c, k_c, v_c, scale).reshape(B, L, D_MODEL) @ w_o
    return o_lat, o_ctx


def ec_moe(h, router_w, w_gate, w_up, w_down):
    B, T, D = h.shape
    cap = EC_FACTOR * T // N_EXPERTS
    aff = jax.nn.softmax((h @ router_w).astype(jnp.float32), axis=-1)
    g, idx = lax.top_k(jnp.swapaxes(aff, 1, 2), cap)
    xs = jax.vmap(lambda hb, ib: hb[ib])(h, idx)
    hid = (jax.nn.silu(jnp.einsum('becd,edf->becf', xs, w_gate))
           * jnp.einsum('becd,edf->becf', xs, w_up))
    y = jnp.einsum('becf,efd->becd', hid, w_down) * g[..., None].astype(h.dtype)
    return jax.vmap(
        lambda yb, ib: jnp.zeros((T, D), yb.dtype).at[ib.reshape(-1)].add(yb.reshape(-1, D)))(y, idx)


def setup_inputs(seed: int = 0) -> dict:
    key = jax.random.key(seed)
    ks = jax.random.split(key, 28)
    f32 = jnp.float32
    n_mla = (DEPTH + 1) // 2
    n_na = DEPTH // 2

    def nrm(k, shape, scale):
        return jax.random.normal(k, shape, f32) * scale

    def gain(k, shape):
        return 1.0 + 0.05 * jax.random.normal(k, shape, f32)

    return {
        'x': nrm(ks[0], (BATCH, SEQ, D_MODEL), 1.0),
        'c': nrm(ks[1], (BATCH, D_MODEL), 1.0),
        'ctx': nrm(ks[2], (BATCH, CTX_LEN, D_MODEL), 1.0),
        'c_ctx': nrm(ks[3], (D_MODEL,), 1.0),
        'ada_w': nrm(ks[4], (DEPTH, D_MODEL, 6 * D_MODEL), 0.5 * D_MODEL ** -0.5),
        'ada_b': nrm(ks[5], (DEPTH, 6 * D_MODEL), 0.02),
        'norm_mix': gain(ks[6], (DEPTH, D_MODEL)),
        'norm_ffn': gain(ks[7], (DEPTH, D_MODEL)),
        'mla_w_in': nrm(ks[8], (n_mla, D_MODEL, Q_LORA + KV_LORA + QK_ROPE), D_MODEL ** -0.5),
        'mla_q_a_gain': gain(ks[9], (n_mla, Q_LORA)),
        'mla_kv_a_gain': gain(ks[10], (n_mla, KV_LORA)),
        'mla_w_qb': nrm(ks[11], (n_mla, Q_LORA, MLA_HEADS * QK_HEAD), Q_LORA ** -0.5),
        'mla_w_kvb': nrm(ks[12], (n_mla, KV_LORA, MLA_HEADS * (QK_NOPE + V_DIM)), KV_LORA ** -0.5),
        'mla_q_gain': gain(ks[13], (n_mla, QK_HEAD)),
        'mla_k_gain': gain(ks[14], (n_mla, QK_HEAD)),
        'mla_w_o': nrm(ks[15], (n_mla, MLA_HEADS * V_DIM, D_MODEL), (MLA_HEADS * V_DIM) ** -0.5),
        'na_w_qkv': nrm(ks[16], (n_na, D_MODEL, 3 * D_MODEL), D_MODEL ** -0.5),
        'na_q_gain': gain(ks[17], (n_na, NA_HEAD_DIM)),
        'na_k_gain': gain(ks[18], (n_na, NA_HEAD_DIM)),
        'na_rpb': nrm(ks[19], (n_na, NA_HEADS, 2 * NA_KH - 1, 2 * NA_KW - 1), 0.1),
        'na_w_o': nrm(ks[20], (n_na, D_MODEL, D_MODEL), D_MODEL ** -0.5),
        'router_w': nrm(ks[21], (DEPTH, D_MODEL, N_EXPERTS), D_MODEL ** -0.5),
        'moe_w_gate': nrm(ks[22], (DEPTH, N_EXPERTS, D_MODEL, EXPERT_FF), D_MODEL ** -0.5),
        'moe_w_up': nrm(ks[23], (DEPTH, N_EXPERTS, D_MODEL, EXPERT_FF), D_MODEL ** -0.5),
        'moe_w_down': nrm(ks[24], (DEPTH, N_EXPERTS, EXPERT_FF, D_MODEL), EXPERT_FF ** -0.5),
    }


def reference(x, c, ctx, c_ctx, ada_w, ada_b, norm_mix, norm_ffn,
              mla_w_in, mla_q_a_gain, mla_kv_a_gain, mla_w_qb, mla_w_kvb, mla_q_gain, mla_k_gain, mla_w_o,
              na_w_qkv, na_q_gain, na_k_gain, na_rpb, na_w_o,
              router_w, moe_w_gate, moe_w_up, moe_w_down):
    B, N, _ = x.shape
    ang_r, ang_c = axial_angles(N, QK_ROPE)
    c_act = jax.nn.silu(c)
    cctx_act = jax.nn.silu(c_ctx)
    x_lat, x_ctx = x, ctx
    for i in range(DEPTH):
        with_ctx = i < DEPTH - 1
        mod_l = (c_act @ ada_w[i] + ada_b[i])[:, None, :]
        mod_c = (cctx_act @ ada_w[i] + ada_b[i])[None, None, :]
        sh_a, sc_a, g_a, sh_f, sc_f, g_f = jnp.split(mod_l, 6, axis=-1)
        csh_a, csc_a, cg_a, csh_f, csc_f, cg_f = jnp.split(mod_c, 6, axis=-1)

        h_lat = modulate(rmsnorm(x_lat, norm_mix[i]), sh_a, sc_a)
        h_ctx = modulate(rmsnorm(x_ctx, norm_mix[i]), csh_a, csc_a)
        j = i // N_MIXERS
        if i % N_MIXERS == 0:
            o_lat, o_ctx = mla_mixer(h_lat, h_ctx, mla_w_in[j], mla_q_a_gain[j], mla_kv_a_gain[j],
                                     mla_w_qb[j], mla_w_kvb[j], mla_q_gain[j], mla_k_gain[j],
                                     mla_w_o[j], ang_r, ang_c, with_ctx)
        else:
            o_lat, o_ctx = na_mixer(h_lat, h_ctx, na_w_qkv[j], na_q_gain[j], na_k_gain[j],
                                    na_rpb[j], na_w_o[j], with_ctx)
        x_lat = x_lat + g_a * o_lat

        f_lat = modulate(rmsnorm(x_lat, norm_ffn[i]), sh_f, sc_f)
        x_lat = x_lat + g_f * ec_moe(f_lat, router_w[i], moe_w_gate[i], moe_w_up[i], moe_w_down[i])

        if with_ctx:
            x_ctx = x_ctx + cg_a * o_ctx
            f_ctx = modulate(rmsnorm(x_ctx, norm_ffn[i]), csh_f, csc_f)
            x_ctx = x_ctx + cg_f * ec_moe(f_ctx, router_w[i], moe_w_gate[i], moe_w_up[i], moe_w_down[i])
    return x_lat
```

```python
import functools
import math

import jax
import jax.numpy as jnp
from jax import lax
from jax.experimental import pallas as pl
from jax.experimental.pallas import tpu as pltpu

F32 = jnp.float32
BF16 = jnp.bfloat16

D_MODEL = 2048
GRID_W = 64
EPS = 1e-6
ROPE_THETA = 10000.0

MLA_HEADS = 16
Q_LORA = 512
KV_LORA = 512
QK_NOPE = 128
QK_ROPE = 64
QK_HEAD = QK_NOPE + QK_ROPE
V_DIM = 128
MLA_QK_PAD = 256

NA_HEADS = 16
NA_HEAD_DIM = 128
NA_KH = 8
NA_KW = 16

N_EXPERTS = 16
EXPERT_FF = 1408
EC_FACTOR = 2

LANES = 128
NEG_BIG = -1e30

VMEM_LIMIT = 56 * 1024 * 1024


def _cparams(sem):
    return pltpu.CompilerParams(dimension_semantics=sem, vmem_limit_bytes=VMEM_LIMIT)


def _ada_kernel(c_ref, w_ref, b_ref, o_ref, *, tn):
    for r in range(2):
        c = c_ref[r]
        act = c * (1.0 / (1.0 + jnp.exp(-c)))
        for k in range(tn // LANES):
            sl = slice(k * LANES, (k + 1) * LANES)
            o_ref[0, r:r + 1, sl] = jnp.sum(act * w_ref[0, :, sl], axis=0, keepdims=True) + b_ref[0, :, sl]


def ada_modulation(cvec, ada_w, ada_b, *, tn=512):
    depth, d, n = ada_w.shape
    c_rep = jnp.broadcast_to(cvec[:, :, None], (2, d, LANES))
    return pl.pallas_call(
        functools.partial(_ada_kernel, tn=tn),
        out_shape=jax.ShapeDtypeStruct((depth, 2, n), F32),
        grid=(depth, n // tn),
        in_specs=[
            pl.BlockSpec((2, d, LANES), lambda l, j: (0, 0, 0)),
            pl.BlockSpec((1, d, tn), lambda l, j: (l, 0, j)),
            pl.BlockSpec((1, 1, tn), lambda l, j: (l, 0, j)),
        ],
        out_specs=pl.BlockSpec((1, 2, tn), lambda l, j: (l, 0, j)),
        compiler_params=_cparams(("arbitrary", "arbitrary")),
        name="ada_modulation",
    )(c_rep, ada_w, ada_b.reshape(depth, 1, n))


def _proj_kernel(x_ref, g_ref, sc_ref, sh_ref, w_ref, cg_ref, o_ref, h_ref, *, tn, n_norm_blocks, head):
    j = pl.program_id(1)

    @pl.when(j == 0)
    def _():
        x = x_ref[...]
        y = x * lax.rsqrt(jnp.mean(x * x, axis=-1, keepdims=True) + EPS) * g_ref[...]
        h_ref[...] = (y * (1.0 + sc_ref[...]) + sh_ref[...]).astype(BF16)

    acc = jnp.dot(h_ref[...], w_ref[...], preferred_element_type=F32)
    if n_norm_blocks == 0:
        o_ref[...] = acc.astype(o_ref.dtype)
    else:
        @pl.when(j < n_norm_blocks)
        def _():
            for c in range(tn // head):
                sl = slice(c * head, (c + 1) * head)
                y = acc[:, sl]
                r = lax.rsqrt(jnp.mean(y * y, axis=-1, keepdims=True) + EPS)
                o_ref[:, sl] = (y * r * cg_ref[:, sl]).astype(o_ref.dtype)

        @pl.when(j >= n_norm_blocks)
        def _():
            o_ref[...] = acc.astype(o_ref.dtype)


def norm_mod_proj(x, gain, scale, shift, w, col_gain, *, tm, tn, out_dtype, n_norm_blocks=0, head=LANES):
    m, d = x.shape
    n = w.shape[1]
    return pl.pallas_call(
        functools.partial(_proj_kernel, tn=tn, n_norm_blocks=n_norm_blocks, head=head),
        out_shape=jax.ShapeDtypeStruct((m, n), out_dtype),
        grid=(m // tm, n // tn),
        in_specs=[
            pl.BlockSpec((tm, d), lambda i, j: (i, 0)),
            pl.BlockSpec((1, d), lambda i, j: (0, 0)),
            pl.BlockSpec((1, d), lambda i, j: (0, 0)),
            pl.BlockSpec((1, d), lambda i, j: (0, 0)),
            pl.BlockSpec((d, tn), lambda i, j: (0, j)),
            pl.BlockSpec((1, tn), lambda i, j: (0, j)),
        ],
        out_specs=pl.BlockSpec((tm, tn), lambda i, j: (i, j)),
        scratch_shapes=[pltpu.VMEM((tm, d), BF16)],
        compiler_params=_cparams(("arbitrary", "arbitrary")),
        name="norm_mod_proj",
    )(x, gain, scale, shift, w, col_gain)


def _mla_qkv_kernel(a_ref, qag_ref, kvag_ref, wq_ref, wkv_ref, gq_ref, gk_ref, rc_ref, rs_ref,
                    q_ref, k_ref, v_ref, *, heads, q_scale):
    a = a_ref[...]
    qc = a[:, :Q_LORA]
    kvc = a[:, Q_LORA:Q_LORA + KV_LORA]
    kpe1 = a[:, Q_LORA + KV_LORA:Q_LORA + KV_LORA + LANES]
    kpe2 = a[:, Q_LORA + KV_LORA + LANES:]

    def rms(x, g):
        return x * lax.rsqrt(jnp.mean(x * x, axis=-1, keepdims=True) + EPS) * g

    qn = rms(qc, qag_ref[...]).astype(BF16)
    kvn = rms(kvc, kvag_ref[...]).astype(BF16)
    rc = rc_ref[...]
    rs = rs_ref[...]
    gq_nope, gq1, gq2 = gq_ref[0:1, :], gq_ref[1:2, :], gq_ref[2:3, :]
    gk_nope, gk1, gk2 = gk_ref[0:1, :], gk_ref[1:2, :], gk_ref[2:3, :]
    kpe_ss = jnp.sum(kpe1 * kpe1, axis=-1, keepdims=True)
    kpe_rot = kpe1 * gk1 * rc + kpe2 * gk2 * rs
    qw = QK_NOPE + 2 * LANES
    for h in range(heads):
        qh = jnp.dot(qn, wq_ref[:, h * qw:(h + 1) * qw], preferred_element_type=F32)
        qa, q1, q2 = qh[:, :QK_NOPE], qh[:, QK_NOPE:QK_NOPE + LANES], qh[:, QK_NOPE + LANES:]
        ss = jnp.sum(qa * qa, axis=-1, keepdims=True) + jnp.sum(q1 * q1, axis=-1, keepdims=True)
        r = lax.rsqrt(ss * (1.0 / QK_HEAD) + EPS) * q_scale
        q_ref[:, h * MLA_QK_PAD:h * MLA_QK_PAD + QK_NOPE] = (qa * r * gq_nope).astype(BF16)
        q_ref[:, h * MLA_QK_PAD + QK_NOPE:(h + 1) * MLA_QK_PAD] = (
            (q1 * gq1 * rc + q2 * gq2 * rs) * r).astype(BF16)
        kvh = jnp.dot(kvn, wkv_ref[:, h * (QK_NOPE + V_DIM):(h + 1) * (QK_NOPE + V_DIM)],
                      preferred_element_type=F32)
        kn, vv = kvh[:, :QK_NOPE], kvh[:, QK_NOPE:]
        rk = lax.rsqrt((jnp.sum(kn * kn, axis=-1, keepdims=True) + kpe_ss) * (1.0 / QK_HEAD) + EPS)
        k_ref[:, h * MLA_QK_PAD:h * MLA_QK_PAD + QK_NOPE] = (kn * rk * gk_nope).astype(BF16)
        k_ref[:, h * MLA_QK_PAD + QK_NOPE:(h + 1) * MLA_QK_PAD] = (kpe_rot * rk).astype(BF16)
        v_ref[:, h * V_DIM:(h + 1) * V_DIM] = vv.astype(BF16)


def mla_qkv(a, q_a_gain, kv_a_gain, wq_ext, wkv, gq, gk, rope_c, rope_s, *, tm, heads=MLA_HEADS):
    m, aw = a.shape
    full = lambda i: (0, 0)
    return pl.pallas_call(
        functools.partial(_mla_qkv_kernel, heads=heads, q_scale=QK_HEAD ** -0.5),
        out_shape=(
            jax.ShapeDtypeStruct((m, heads * MLA_QK_PAD), BF16),
            jax.ShapeDtypeStruct((m, heads * MLA_QK_PAD), BF16),
            jax.ShapeDtypeStruct((m, heads * V_DIM), BF16),
        ),
        grid=(m // tm,),
        in_specs=[
            pl.BlockSpec((tm, aw), lambda i: (i, 0)),
            pl.BlockSpec((1, Q_LORA), full),
            pl.BlockSpec((1, KV_LORA), full),
            pl.BlockSpec(wq_ext.shape, full),
            pl.BlockSpec(wkv.shape, full),
            pl.BlockSpec((8, LANES), full),
            pl.BlockSpec((8, LANES), full),
            pl.BlockSpec((tm, LANES), lambda i: (i, 0)),
            pl.BlockSpec((tm, LANES), lambda i: (i, 0)),
        ],
        out_specs=(
            pl.BlockSpec((tm, heads * MLA_QK_PAD), lambda i: (i, 0)),
            pl.BlockSpec((tm, heads * MLA_QK_PAD), lambda i: (i, 0)),
            pl.BlockSpec((tm, heads * V_DIM), lambda i: (i, 0)),
        ),
        compiler_params=_cparams(("arbitrary",)),
        name="mla_qkv",
    )(a, q_a_gain, kv_a_gain, wq_ext, wkv, gq, gk, rope_c, rope_s)


def _nt_dot(a, b):
    return lax.dot_general(a, b, (((1,), (1,)), ((), ())), preferred_element_type=F32)


def _mla_attn_kernel(*refs, tk, n_lat):
    if n_lat:
        q_ref, kc_ref, vc_ref, kl_ref, vl_ref, o_ref = refs
    else:
        q_ref, kc_ref, vc_ref, o_ref = refs
    q = q_ref[...]

    def step(k, v, m, l, acc):
        s = _nt_dot(q, k)
        m_new = jnp.maximum(m, jnp.max(s, axis=-1, keepdims=True))
        p = jnp.exp(s - m_new)
        alpha = jnp.exp(m - m_new)
        l_new = alpha * l + jnp.sum(p, axis=-1, keepdims=True)
        acc_new = alpha * acc + jnp.dot(p.astype(BF16), v, preferred_element_type=F32)
        return m_new, l_new, acc_new

    tq = q.shape[0]
    m0 = jnp.full((tq, 1), NEG_BIG, F32)
    l0 = jnp.zeros((tq, 1), F32)
    acc0 = jnp.zeros((tq, V_DIM), F32)
    m, l, acc = step(kc_ref[...], vc_ref[...], m0, l0, acc0)
    if n_lat:
        def body(i, carry):
            start = pl.multiple_of(i * tk, tk)
            return step(kl_ref[pl.ds(start, tk), :], vl_ref[pl.ds(start, tk), :], *carry)

        m, l, acc = lax.fori_loop(0, n_lat // tk, body, (m, l, acc))
    o_ref[...] = (acc * (1.0 / l)).astype(o_ref.dtype)


def mla_attention(q, k_ctx, v_ctx, k_lat=None, v_lat=None, *, tq, tk=512, heads=MLA_HEADS):
    nq = q.shape[0]
    n_ctx = k_ctx.shape[0]
    n_lat = 0 if k_lat is None else k_lat.shape[0]
    in_specs = [
        pl.BlockSpec((tq, MLA_QK_PAD), lambda h, i: (i, h)),
        pl.BlockSpec((n_ctx, MLA_QK_PAD), lambda h, i: (0, h)),
        pl.BlockSpec((n_ctx, V_DIM), lambda h, i: (0, h)),
    ]
    args = [q, k_ctx, v_ctx]
    if n_lat:
        in_specs += [
            pl.BlockSpec((n_lat, MLA_QK_PAD), lambda h, i: (0, h)),
            pl.BlockSpec((n_lat, V_DIM), lambda h, i: (0, h)),
        ]
        args += [k_lat, v_lat]
    return pl.pallas_call(
        functools.partial(_mla_attn_kernel, tk=tk, n_lat=n_lat),
        out_shape=jax.ShapeDtypeStruct((nq, heads * V_DIM), BF16),
        grid=(heads, nq // tq),
        in_specs=in_specs,
        out_specs=pl.BlockSpec((tq, V_DIM), lambda h, i: (i, h)),
        compiler_params=_cparams(("arbitrary", "arbitrary")),
        name="mla_attention",
    )(*args)


def _na_attn_kernel(q_ref, k_ref, v_ref, kc_ref, vc_ref, b_ref, o_ref, *, rb, span_rows, rows):
    i = pl.program_id(1)
    kr0 = jnp.clip(i * rb - NA_KH // 2, 0, rows - span_rows)
    start = pl.multiple_of(kr0 * GRID_W, (NA_KH // 2) * GRID_W)
    span = span_rows * GRID_W
    q = q_ref[...]
    kw = k_ref[pl.ds(start, span), :]
    vw = v_ref[pl.ds(start, span), :]
    s_win = _nt_dot(q, kw) + b_ref[0, 0]
    s_ctx = _nt_dot(q, kc_ref[...])
    m = jnp.maximum(jnp.max(s_win, axis=-1, keepdims=True), jnp.max(s_ctx, axis=-1, keepdims=True))
    p_win = jnp.exp(s_win - m)
    p_ctx = jnp.exp(s_ctx - m)
    l = jnp.sum(p_win, axis=-1, keepdims=True) + jnp.sum(p_ctx, axis=-1, keepdims=True)
    o = (jnp.dot(p_win.astype(BF16), vw, preferred_element_type=F32)
         + jnp.dot(p_ctx.astype(BF16), vc_ref[...], preferred_element_type=F32))
    o_ref[...] = (o * (1.0 / l)).astype(o_ref.dtype)


def na_attention(qkv, qkv_ctx, bias, *, rb=8, heads=NA_HEADS):
    n = qkv.shape[0]
    n_ctx = qkv_ctx.shape[0]
    rows = n // GRID_W
    span_rows = rb + NA_KH
    nblk = rows // rb
    tq = rb * GRID_W

    def bias_map(h, i):
        return (h, jnp.where(i == 0, 0, jnp.where(i == nblk - 1, 2, 1)), 0, 0)

    return pl.pallas_call(
        functools.partial(_na_attn_kernel, rb=rb, span_rows=span_rows, rows=rows),
        out_shape=jax.ShapeDtypeStruct((n, heads * NA_HEAD_DIM), BF16),
        grid=(heads, nblk),
        in_specs=[
            pl.BlockSpec((tq, NA_HEAD_DIM), lambda h, i: (i, h)),
            pl.BlockSpec((n, NA_HEAD_DIM), lambda h, i: (0, heads + h)),
            pl.BlockSpec((n, NA_HEAD_DIM), lambda h, i: (0, 2 * heads + h)),
            pl.BlockSpec((n_ctx, NA_HEAD_DIM), lambda h, i: (0, heads + h)),
            pl.BlockSpec((n_ctx, NA_HEAD_DIM), lambda h, i: (0, 2 * heads + h)),
            pl.BlockSpec((1, 1, tq, span_rows * GRID_W), bias_map),
        ],
        out_specs=pl.BlockSpec((tq, NA_HEAD_DIM), lambda h, i: (i, h)),
        compiler_params=_cparams(("arbitrary", "arbitrary")),
        name="na_attention",
    )(qkv, qkv, qkv, qkv_ctx, qkv_ctx, bias)


def na_bias_tables(rpb, *, rows, rb=8):
    span_rows = rb + NA_KH
    nblk = rows // rb
    qr = jnp.arange(rb)[:, None, None, None]
    qc = jnp.arange(GRID_W)[None, :, None, None]
    kr = jnp.arange(span_rows)[None, None, :, None]
    kc = jnp.arange(GRID_W)[None, None, None, :]
    col_start = jnp.clip(qc - NA_KW // 2, 0, GRID_W - NA_KW)
    col_ok = (kc >= col_start) & (kc < col_start + NA_KW)
    dc = jnp.clip(kc - qc, -(NA_KW - 1), NA_KW - 1) + NA_KW - 1
    tables = []
    for blk in (0, 1, nblk - 1):
        r_abs = blk * rb + qr
        kr0 = min(max(blk * rb - NA_KH // 2, 0), rows - span_rows)
        k_abs = kr0 + kr
        r0 = jnp.clip(r_abs - NA_KH // 2, 0, rows - NA_KH)
        row_ok = (k_abs >= r0) & (k_abs < r0 + NA_KH)
        dr = jnp.clip(k_abs - r_abs + NA_KH - 1, 0, 2 * NA_KH - 2)
        ok = jnp.broadcast_to(row_ok & col_ok, (rb, GRID_W, span_rows, GRID_W))
        drb = jnp.broadcast_to(dr, ok.shape)
        dcb = jnp.broadcast_to(dc, ok.shape)
        t = jnp.where(ok[None], rpb[:, drb, dcb], NEG_BIG)
        tables.append(t.reshape(rpb.shape[0], rb * GRID_W, span_rows * GRID_W))
    return jnp.stack(tables, axis=1).astype(F32)


def _outproj_kernel(o_ref, w_ref, x_ref, ga_ref, g_ref, sc_ref, sh_ref, rwh_ref, rwl_ref,
                    xo_ref, f_ref, aff_ref, *, n_experts):
    acc = jnp.dot(o_ref[...], w_ref[...], preferred_element_type=F32)
    xn = x_ref[...] + ga_ref[...] * acc
    xo_ref[...] = xn
    y = xn * lax.rsqrt(jnp.mean(xn * xn, axis=-1, keepdims=True) + EPS) * g_ref[...]
    f = y * (1.0 + sc_ref[...]) + sh_ref[...]
    f_ref[...] = f
    f_hi = f.astype(BF16)
    f_lo = (f - f_hi.astype(F32)).astype(BF16)
    logits = (jnp.dot(f_hi, rwh_ref[...], preferred_element_type=F32)
              + jnp.dot(f_lo, rwh_ref[...], preferred_element_type=F32)
              + jnp.dot(f_hi, rwl_ref[...], preferred_element_type=F32))
    lane = lax.broadcasted_iota(jnp.int32, logits.shape, 1)
    logits = jnp.where(lane < n_experts, logits, NEG_BIG)
    e = jnp.exp(logits - jnp.max(logits, axis=-1, keepdims=True))
    aff_ref[...] = e * (1.0 / jnp.sum(e, axis=-1, keepdims=True))


def outproj_ffnprep(o, w_o, x, gate_a, gain, scale, shift, rw_hi, rw_lo, *, tm, n_experts=N_EXPERTS):
    m, d = x.shape
    k = o.shape[1]
    row = lambda i: (i, 0)
    full = lambda i: (0, 0)
    return pl.pallas_call(
        functools.partial(_outproj_kernel, n_experts=n_experts),
        out_shape=(
            jax.ShapeDtypeStruct((m, d), F32),
            jax.ShapeDtypeStruct((m, d), F32),
            jax.ShapeDtypeStruct((m, LANES), F32),
        ),
        grid=(m // tm,),
        in_specs=[
            pl.BlockSpec((tm, k), row),
            pl.BlockSpec((k, d), full),
            pl.BlockSpec((tm, d), row),
            pl.BlockSpec((1, d), full),
            pl.BlockSpec((1, d), full),
            pl.BlockSpec((1, d), full),
            pl.BlockSpec((1, d), full),
            pl.BlockSpec((d, LANES), full),
            pl.BlockSpec((d, LANES), full),
        ],
        out_specs=(
            pl.BlockSpec((tm, d), row),
            pl.BlockSpec((tm, d), row),
            pl.BlockSpec((tm, LANES), row),
        ),
        compiler_params=_cparams(("arbitrary",)),
        name="outproj_ffnprep",
    )(o, w_o, x, gate_a, gain, scale, shift, rw_hi, rw_lo)


def _prefix_rank(sel_bf, upper_incl, lower_strict):
    w = jnp.dot(sel_bf, upper_incl, preferred_element_type=F32)
    row_tot = jnp.broadcast_to(w[:, LANES - 1:LANES], w.shape)
    row_off = jnp.dot(lower_strict, row_tot.astype(BF16), preferred_element_type=F32)
    return w, row_tot, row_off


def _topk_kernel(a_ref, tok_ref, gate_ref, *, cap, jc):
    a = a_ref[0]
    tb = a.shape[0]
    ri = lax.broadcasted_iota(jnp.int32, (tb, LANES), 0)
    ci = lax.broadcasted_iota(jnp.int32, (tb, LANES), 1)
    upper_incl = (ri <= ci).astype(BF16)
    lower_strict = (ci < ri).astype(BF16)

    def count(mask):
        c = jnp.sum(mask.astype(F32), axis=1, keepdims=True)
        return jnp.sum(c, axis=0, keepdims=True)

    thr = jnp.zeros((1, 1), jnp.int32)
    for bit in range(30, -1, -1):
        cand = thr | jnp.int32(1 << bit)
        candf = lax.bitcast_convert_type(cand, F32)
        thr = jnp.where(count(a >= candf) >= cap, cand, thr)
    thrf = lax.bitcast_convert_type(thr, F32)
    gt = a > thrf
    eq = a == thrf
    need = cap - count(gt)
    w_eq, _, off_eq = _prefix_rank(eq.astype(BF16), upper_incl, lower_strict)
    sel = gt | (eq & (w_eq + off_eq <= need))
    sel_bf = sel.astype(BF16)
    w, row_tot, row_off = _prefix_rank(sel_bf, upper_incl, lower_strict)
    tot_t = row_tot.T
    incl_t = jnp.dot(tot_t.astype(BF16), upper_incl, preferred_element_type=F32)
    excl_t = incl_t - tot_t
    excl_row = excl_t[0:1, :]
    incl_row = incl_t[0:1, :]
    w_bf = w.astype(BF16)
    a1 = a.astype(BF16)
    a2 = (a - a1.astype(F32)).astype(BF16)
    a3 = (a - a1.astype(F32) - a2.astype(F32)).astype(BF16)
    lane_f = lax.broadcasted_iota(jnp.int32, (jc, LANES), 1).astype(F32)
    for c in range(cap // jc):
        slot = (lax.broadcasted_iota(jnp.int32, (jc, 1), 0) + c * jc).astype(F32)
        in_row = (excl_row <= slot) & (slot < incl_row)
        in_row_f = in_row.astype(F32)
        in_row_bf = in_row.astype(BF16)
        w_g = jnp.dot(in_row_bf, w_bf, preferred_element_type=F32)
        s_g = jnp.dot(in_row_bf, sel_bf, preferred_element_type=F32)
        off_g = jnp.sum(in_row_f * excl_row, axis=1, keepdims=True)
        hit = (s_g > 0.5) & (w_g + off_g == slot + 1.0)
        hit_f = hit.astype(F32)
        row_id = jnp.sum(in_row_f * lane_f, axis=1, keepdims=True)
        lane_id = jnp.sum(hit_f * lane_f, axis=1, keepdims=True)
        a_g = (jnp.dot(in_row_bf, a1, preferred_element_type=F32)
               + jnp.dot(in_row_bf, a2, preferred_element_type=F32)
               + jnp.dot(in_row_bf, a3, preferred_element_type=F32))
        tok_ref[0, c * jc:(c + 1) * jc, :] = (row_id * float(LANES) + lane_id).astype(jnp.int32)
        gate_ref[0, c * jc:(c + 1) * jc, :] = jnp.sum(hit_f * a_g, axis=1, keepdims=True)


def expert_choice(aff_t, *, cap):
    e, t = aff_t.shape
    t_pad = LANES * LANES
    a = jnp.pad(aff_t, ((0, 0), (0, t_pad - t)), constant_values=-1.0).reshape(e, LANES, LANES)
    jc = min(cap, 256)
    tok, gate = pl.pallas_call(
        functools.partial(_topk_kernel, cap=cap, jc=jc),
        out_shape=(
            jax.ShapeDtypeStruct((e, cap, 1), jnp.int32),
            jax.ShapeDtypeStruct((e, cap, 1), F32),
        ),
        grid=(e,),
        in_specs=[pl.BlockSpec((1, LANES, LANES), lambda i: (i, 0, 0))],
        out_specs=(
            pl.BlockSpec((1, cap, 1), lambda i: (i, 0, 0)),
            pl.BlockSpec((1, cap, 1), lambda i: (i, 0, 0)),
        ),
        compiler_params=_cparams(("arbitrary",)),
        name="expert_choice",
    )(a)
    return tok.reshape(e, cap), gate


def _moe_kernel(idx_hbm, f_hbm, x_hbm, g_ref, gf_ref, wg_ref, wu_ref, wd_ref, o_hbm,
                idx_s, xs, xo, sems, *, tm, chunks):
    del x_hbm
    step = pl.program_id(0) * chunks + pl.program_id(1)
    idx_copy = pltpu.make_async_copy(idx_hbm.at[step], idx_s, sems.at[0])
    idx_copy.start()
    idx_copy.wait()

    def f_row(j):
        return pltpu.make_async_copy(f_hbm.at[pl.ds(idx_s[j], 1)], xs.at[pl.ds(j, 1)], sems.at[1])

    def x_row_in(j):
        return pltpu.make_async_copy(o_hbm.at[pl.ds(idx_s[j], 1)], xo.at[pl.ds(j, 1)], sems.at[2])

    def x_row_out(j):
        return pltpu.make_async_copy(xo.at[pl.ds(j, 1)], o_hbm.at[pl.ds(idx_s[j], 1)], sems.at[3])

    def start_gather(j, c):
        f_row(j).start()
        x_row_in(j).start()
        return c

    def wait_gather(j, c):
        f_row(j).wait()
        x_row_in(j).wait()
        return c

    lax.fori_loop(0, tm, start_gather, 0)
    lax.fori_loop(0, tm, wait_gather, 0)
    xb = xs[...].astype(BF16)
    hg = jnp.dot(xb, wg_ref[0], preferred_element_type=F32)
    hu = jnp.dot(xb, wu_ref[0], preferred_element_type=F32)
    hid = (hg * (1.0 / (1.0 + jnp.exp(-hg)))) * hu
    y = jnp.dot(hid.astype(BF16), wd_ref[0], preferred_element_type=F32)
    xo[...] = xo[...] + gf_ref[...] * (y * g_ref[...])

    def start_scatter(j, c):
        x_row_out(j).start()
        return c

    def wait_scatter(j, c):
        x_row_out(j).wait()
        return c

    lax.fori_loop(0, tm, start_scatter, 0)
    lax.fori_loop(0, tm, wait_scatter, 0)


def moe_apply(f, x, tok, gate, gate_f, wg, wu, wd, *, tm):
    t, d = x.shape
    e, cap = tok.shape
    chunks = cap // tm
    ff = wg.shape[2]
    return pl.pallas_call(
        functools.partial(_moe_kernel, tm=tm, chunks=chunks),
        out_shape=jax.ShapeDtypeStruct((t, d), F32),
        grid=(e, chunks),
        in_specs=[
            pl.BlockSpec(memory_space=pl.ANY),
            pl.BlockSpec(memory_space=pl.ANY),
            pl.BlockSpec(memory_space=pl.ANY),
            pl.BlockSpec((tm, 1), lambda i, c: (i * chunks + c, 0)),
            pl.BlockSpec((1, d), lambda i, c: (0, 0)),
            pl.BlockSpec((1, d, ff), lambda i, c: (i, 0, 0)),
            pl.BlockSpec((1, d, ff), lambda i, c: (i, 0, 0)),
            pl.BlockSpec((1, ff, d), lambda i, c: (i, 0, 0)),
        ],
        out_specs=pl.BlockSpec(memory_space=pl.ANY),
        scratch_shapes=[
            pltpu.SMEM((tm,), jnp.int32),
            pltpu.VMEM((tm, d), F32),
            pltpu.VMEM((tm, d), F32),
            pltpu.SemaphoreType.DMA((4,)),
        ],
        input_output_aliases={2: 0},
        compiler_params=_cparams(("arbitrary", "arbitrary")),
        name="moe_apply",
    )(tok.reshape(e * chunks, tm), f, x, gate.reshape(e * cap, 1), gate_f, wg, wu, wd)


def _swap_rot_halves(w):
    q = QK_ROPE // 4
    return jnp.concatenate([w[..., q:2 * q], w[..., :q], w[..., 3 * q:], w[..., 2 * q:3 * q]], axis=-1)


def _pad_lanes(w):
    return jnp.pad(w, [(0, 0)] * (w.ndim - 1) + [(0, LANES - w.shape[-1])])


def _rope_tables(n):
    half = QK_ROPE // 2
    inv = ROPE_THETA ** (-jnp.arange(0, half, 2, dtype=F32) / half)
    t = jnp.arange(n)
    ang_r = (t // GRID_W).astype(F32)[:, None] * inv[None, :]
    ang_c = (t % GRID_W).astype(F32)[:, None] * inv[None, :]
    cr, sr, cc, sc = jnp.cos(ang_r), jnp.sin(ang_r), jnp.cos(ang_c), jnp.sin(ang_c)
    return (_pad_lanes(jnp.concatenate([cr, cr, cc, cc], axis=-1)),
            _pad_lanes(jnp.concatenate([-sr, sr, -sc, sc], axis=-1)))


def _gain_rows(g):
    pe = g[QK_NOPE:]
    rows = jnp.stack([g[:QK_NOPE], _pad_lanes(pe), _pad_lanes(_swap_rot_halves(pe))])
    return jnp.pad(rows, ((0, 5), (0, 0)))


def _moe_layer(f, x, aff, gate_f, wg, wu, wd, *, tm):
    t = x.shape[0]
    cap = EC_FACTOR * t // N_EXPERTS
    tok, gate = expert_choice(aff[:, :N_EXPERTS].T, cap=cap)
    return moe_apply(f, x, tok, gate, gate_f, wg, wu, wd, tm=min(tm, cap))


def kernel(x, c, ctx, c_ctx, ada_w, ada_b, norm_mix, norm_ffn, mla_w_in, mla_q_a_gain, mla_kv_a_gain,
           mla_w_qb, mla_w_kvb, mla_q_gain, mla_k_gain, mla_w_o, na_w_qkv, na_q_gain, na_k_gain, na_rpb,
           na_w_o, router_w, moe_w_gate, moe_w_up, moe_w_down):
    n = x.shape[1]
    n_ctx = ctx.shape[1]
    d = D_MODEL
    x_lat = x[0]
    x_ctx = ctx[0]

    mods = ada_modulation(jnp.stack([c[0], c_ctx]), ada_w, ada_b)

    def mod(layer, r, k):
        return mods[layer, r:r + 1, k * d:(k + 1) * d]

    row = lambda v: v.reshape(1, -1)
    rw_hi = [_pad_lanes(router_w[i]).astype(BF16) for i in range(2)]
    rw_lo = [(_pad_lanes(router_w[i]) - rw_hi[i].astype(F32)).astype(BF16) for i in range(2)]
    wg = moe_w_gate.astype(BF16)
    wu = moe_w_up.astype(BF16)
    wd = moe_w_down.astype(BF16)

    w_in = mla_w_in[0]
    kpe_w = w_in[:, Q_LORA + KV_LORA:]
    w_in_ext = jnp.concatenate(
        [w_in[:, :Q_LORA + KV_LORA], _pad_lanes(kpe_w), _pad_lanes(_swap_rot_halves(kpe_w))], axis=1).astype(BF16)
    wq = mla_w_qb[0].reshape(Q_LORA, MLA_HEADS, QK_HEAD)
    wq_ext = jnp.concatenate(
        [wq[..., :QK_NOPE], _pad_lanes(wq[..., QK_NOPE:]), _pad_lanes(_swap_rot_halves(wq[..., QK_NOPE:]))],
        axis=-1).reshape(Q_LORA, -1).astype(BF16)
    wkv = mla_w_kvb[0].astype(BF16)
    gq = _gain_rows(mla_q_gain[0])
    gk = _gain_rows(mla_k_gain[0])
    rope_c, rope_s = _rope_tables(n)
    ones_c = _pad_lanes(jnp.ones((n_ctx, QK_ROPE), F32))
    zeros_s = jnp.zeros((n_ctx, LANES), F32)
    no_gain = jnp.ones((1, w_in_ext.shape[1]), F32)

    a_lat = norm_mod_proj(x_lat, row(norm_mix[0]), mod(0, 0, 1), mod(0, 0, 0), w_in_ext, no_gain,
                          tm=512, tn=w_in_ext.shape[1], out_dtype=F32)
    a_ctx = norm_mod_proj(x_ctx, row(norm_mix[0]), mod(0, 1, 1), mod(0, 1, 0), w_in_ext, no_gain,
                          tm=n_ctx, tn=w_in_ext.shape[1], out_dtype=F32)
    q_l, k_l, v_l = mla_qkv(a_lat, row(mla_q_a_gain[0]), row(mla_kv_a_gain[0]), wq_ext, wkv, gq, gk,
                            rope_c, rope_s, tm=256)
    q_c, k_c, v_c = mla_qkv(a_ctx, row(mla_q_a_gain[0]), row(mla_kv_a_gain[0]), wq_ext, wkv, gq, gk,
                            ones_c, zeros_s, tm=n_ctx)
    o_lat = mla_attention(q_l, k_c, v_c, k_l, v_l, tq=512)
    o_ctx = mla_attention(q_c, k_c, v_c, tq=n_ctx)

    w_o = mla_w_o[0].astype(BF16)
    x_lat, f_lat, aff_lat = outproj_ffnprep(o_lat, w_o, x_lat, mod(0, 0, 2), row(norm_ffn[0]), mod(0, 0, 4),
                                            mod(0, 0, 3), rw_hi[0], rw_lo[0], tm=512)
    x_ctx, f_ctx, aff_ctx = outproj_ffnprep(o_ctx, w_o, x_ctx, mod(0, 1, 2), row(norm_ffn[0]), mod(0, 1, 4),
                                            mod(0, 1, 3), rw_hi[0], rw_lo[0], tm=n_ctx)
    x_lat = _moe_layer(f_lat, x_lat, aff_lat, mod(0, 0, 5), wg[0], wu[0], wd[0], tm=256)
    x_ctx = _moe_layer(f_ctx, x_ctx, aff_ctx, mod(0, 1, 5), wg[0], wu[0], wd[0], tm=256)

    w_qkv = na_w_qkv[0].astype(BF16)
    qkv_gain = jnp.concatenate([jnp.tile(na_q_gain[0], NA_HEADS) * NA_HEAD_DIM ** -0.5,
                                jnp.tile(na_k_gain[0], NA_HEADS), jnp.ones((d,), F32)]).reshape(1, -1)
    qkv_l = norm_mod_proj(x_lat, row(norm_mix[1]), mod(1, 0, 1), mod(1, 0, 0), w_qkv, qkv_gain,
                          tm=512, tn=1024, out_dtype=BF16, n_norm_blocks=4)
    qkv_c = norm_mod_proj(x_ctx, row(norm_mix[1]), mod(1, 1, 1), mod(1, 1, 0), w_qkv, qkv_gain,
                          tm=n_ctx, tn=1024, out_dtype=BF16, n_norm_blocks=4)
    bias = na_bias_tables(na_rpb[0], rows=n // GRID_W)
    o_lat = na_attention(qkv_l, qkv_c, bias)
    x_lat, f_lat, aff_lat = outproj_ffnprep(o_lat, na_w_o[0].astype(BF16), x_lat, mod(1, 0, 2), row(norm_ffn[1]),
                                            mod(1, 0, 4), mod(1, 0, 3), rw_hi[1], rw_lo[1], tm=512)
    x_lat = _moe_layer(f_lat, x_lat, aff_lat, mod(1, 0, 5), wg[1], wu[1], wd[1], tm=256)
    return x_lat[None]
```

```python
import functools
import math

import jax
import jax.numpy as jnp
from jax import lax
from jax.experimental import pallas as pl
from jax.experimental.pallas import tpu as pltpu

F32 = jnp.float32
BF16 = jnp.bfloat16

D_MODEL = 2048
GRID_W = 64
EPS = 1e-6
ROPE_THETA = 10000.0

MLA_HEADS = 16
Q_LORA = 512
KV_LORA = 512
QK_NOPE = 128
QK_ROPE = 64
QK_HEAD = QK_NOPE + QK_ROPE
V_DIM = 128
MLA_QK_PAD = 256

NA_HEADS = 16
NA_HEAD_DIM = 128
NA_KH = 8
NA_KW = 16

N_EXPERTS = 16
EXPERT_FF = 1408
EC_FACTOR = 2

LANES = 128
NEG_BIG = -1e30

VMEM_LIMIT = 56 * 1024 * 1024


def _cparams(sem):
    return pltpu.CompilerParams(dimension_semantics=sem, vmem_limit_bytes=VMEM_LIMIT)


def _ada_kernel(c_ref, w_ref, b_ref, o_ref, *, tn):
    for r in range(2):
        c = c_ref[r]
        act = c * (1.0 / (1.0 + jnp.exp(-c)))
        for k in range(tn // LANES):
            sl = slice(k * LANES, (k + 1) * LANES)
            o_ref[0, r:r + 1, sl] = jnp.sum(act * w_ref[0, :, sl], axis=0, keepdims=True) + b_ref[0, :, sl]


def ada_modulation(cvec, ada_w, ada_b, *, tn=512):
    depth, d, n = ada_w.shape
    c_rep = jnp.broadcast_to(cvec[:, :, None], (2, d, LANES))
    return pl.pallas_call(
        functools.partial(_ada_kernel, tn=tn),
        out_shape=jax.ShapeDtypeStruct((depth, 2, n), F32),
        grid=(depth, n // tn),
        in_specs=[
            pl.BlockSpec((2, d, LANES), lambda l, j: (0, 0, 0)),
            pl.BlockSpec((1, d, tn), lambda l, j: (l, 0, j)),
            pl.BlockSpec((1, 1, tn), lambda l, j: (l, 0, j)),
        ],
        out_specs=pl.BlockSpec((1, 2, tn), lambda l, j: (l, 0, j)),
        compiler_params=_cparams(("arbitrary", "arbitrary")),
        name="ada_modulation",
    )(c_rep, ada_w, ada_b.reshape(depth, 1, n))


def _proj_kernel(x_ref, g_ref, sc_ref, sh_ref, w_ref, cg_ref, o_ref, h_ref, *, tn, n_norm_blocks, head):
    j = pl.program_id(1)

    @pl.when(j == 0)
    def _():
        x = x_ref[...]
        y = x * lax.rsqrt(jnp.mean(x * x, axis=-1, keepdims=True) + EPS) * g_ref[...]
        h_ref[...] = (y * (1.0 + sc_ref[...]) + sh_ref[...]).astype(BF16)

    acc = jnp.dot(h_ref[...], w_ref[...], preferred_element_type=F32)
    if n_norm_blocks == 0:
        o_ref[...] = acc.astype(o_ref.dtype)
    else:
        @pl.when(j < n_norm_blocks)
        def _():
            for c in range(tn // head):
                sl = slice(c * head, (c + 1) * head)
                y = acc[:, sl]
                r = lax.rsqrt(jnp.mean(y * y, axis=-1, keepdims=True) + EPS)
                o_ref[:, sl] = (y * r * cg_ref[:, sl]).astype(o_ref.dtype)

        @pl.when(j >= n_norm_blocks)
        def _():
            o_ref[...] = acc.astype(o_ref.dtype)


def norm_mod_proj(x, gain, scale, shift, w, col_gain, *, tm, tn, out_dtype, n_norm_blocks=0, head=LANES):
    m, d = x.shape
    n = w.shape[1]
    return pl.pallas_call(
        functools.partial(_proj_kernel, tn=tn, n_norm_blocks=n_norm_blocks, head=head),
        out_shape=jax.ShapeDtypeStruct((m, n), out_dtype),
        grid=(m // tm, n // tn),
        in_specs=[
            pl.BlockSpec((tm, d), lambda i, j: (i, 0)),
            pl.BlockSpec((1, d), lambda i, j: (0, 0)),
            pl.BlockSpec((1, d), lambda i, j: (0, 0)),
            pl.BlockSpec((1, d), lambda i, j: (0, 0)),
            pl.BlockSpec((d, tn), lambda i, j: (0, j)),
            pl.BlockSpec((1, tn), lambda i, j: (0, j)),
        ],
        out_specs=pl.BlockSpec((tm, tn), lambda i, j: (i, j)),
        scratch_shapes=[pltpu.VMEM((tm, d), BF16)],
        compiler_params=_cparams(("arbitrary", "arbitrary")),
        name="norm_mod_proj",
    )(x, gain, scale, shift, w, col_gain)


def _mla_qkv_kernel(a_ref, qag_ref, kvag_ref, wq_ref, wkv_ref, gq_ref, gk_ref, rc_ref, rs_ref,
                    q_ref, k_ref, v_ref, *, heads, q_scale):
    a = a_ref[...]
    qc = a[:, :Q_LORA]
    kvc = a[:, Q_LORA:Q_LORA + KV_LORA]
    kpe1 = a[:, Q_LORA + KV_LORA:Q_LORA + KV_LORA + LANES]
    kpe2 = a[:, Q_LORA + KV_LORA + LANES:]

    def rms(x, g):
        return x * lax.rsqrt(jnp.mean(x * x, axis=-1, keepdims=True) + EPS) * g

    qn = rms(qc, qag_ref[...]).astype(BF16)
    kvn = rms(kvc, kvag_ref[...]).astype(BF16)
    rc = rc_ref[...]
    rs = rs_ref[...]
    gq_nope, gq1, gq2 = gq_ref[0:1, :], gq_ref[1:2, :], gq_ref[2:3, :]
    gk_nope, gk1, gk2 = gk_ref[0:1, :], gk_ref[1:2, :], gk_ref[2:3, :]
    kpe_ss = jnp.sum(kpe1 * kpe1, axis=-1, keepdims=True)
    kpe_rot = kpe1 * gk1 * rc + kpe2 * gk2 * rs
    qw = QK_NOPE + 2 * LANES
    for h in range(heads):
        qh = jnp.dot(qn, wq_ref[:, h * qw:(h + 1) * qw], preferred_element_type=F32)
        qa, q1, q2 = qh[:, :QK_NOPE], qh[:, QK_NOPE:QK_NOPE + LANES], qh[:, QK_NOPE + LANES:]
        ss = jnp.sum(qa * qa, axis=-1, keepdims=True) + jnp.sum(q1 * q1, axis=-1, keepdims=True)
        r = lax.rsqrt(ss * (1.0 / QK_HEAD) + EPS) * q_scale
        q_ref[:, h * MLA_QK_PAD:h * MLA_QK_PAD + QK_NOPE] = (qa * r * gq_nope).astype(BF16)
        q_ref[:, h * MLA_QK_PAD + QK_NOPE:(h + 1) * MLA_QK_PAD] = (
            (q1 * gq1 * rc + q2 * gq2 * rs) * r).astype(BF16)
        kvh = jnp.dot(kvn, wkv_ref[:, h * (QK_NOPE + V_DIM):(h + 1) * (QK_NOPE + V_DIM)],
                      preferred_element_type=F32)
        kn, vv = kvh[:, :QK_NOPE], kvh[:, QK_NOPE:]
        rk = lax.rsqrt((jnp.sum(kn * kn, axis=-1, keepdims=True) + kpe_ss) * (1.0 / QK_HEAD) + EPS)
        k_ref[:, h * MLA_QK_PAD:h * MLA_QK_PAD + QK_NOPE] = (kn * rk * gk_nope).astype(BF16)
        k_ref[:, h * MLA_QK_PAD + QK_NOPE:(h + 1) * MLA_QK_PAD] = (kpe_rot * rk).astype(BF16)
        v_ref[:, h * V_DIM:(h + 1) * V_DIM] = vv.astype(BF16)


def mla_qkv(a, q_a_gain, kv_a_gain, wq_ext, wkv, gq, gk, rope_c, rope_s, *, tm, heads=MLA_HEADS):
    m, aw = a.shape
    full = lambda i: (0, 0)
    return pl.pallas_call(
        functools.partial(_mla_qkv_kernel, heads=heads, q_scale=QK_HEAD ** -0.5 * math.log2(math.e)),
        out_shape=(
            jax.ShapeDtypeStruct((m, heads * MLA_QK_PAD), BF16),
            jax.ShapeDtypeStruct((m, heads * MLA_QK_PAD), BF16),
            jax.ShapeDtypeStruct((m, heads * V_DIM), BF16),
        ),
        grid=(m // tm,),
        in_specs=[
            pl.BlockSpec((tm, aw), lambda i: (i, 0)),
            pl.BlockSpec((1, Q_LORA), full),
            pl.BlockSpec((1, KV_LORA), full),
            pl.BlockSpec(wq_ext.shape, full),
            pl.BlockSpec(wkv.shape, full),
            pl.BlockSpec((8, LANES), full),
            pl.BlockSpec((8, LANES), full),
            pl.BlockSpec((tm, LANES), lambda i: (i, 0)),
            pl.BlockSpec((tm, LANES), lambda i: (i, 0)),
        ],
        out_specs=(
            pl.BlockSpec((tm, heads * MLA_QK_PAD), lambda i: (i, 0)),
            pl.BlockSpec((tm, heads * MLA_QK_PAD), lambda i: (i, 0)),
            pl.BlockSpec((tm, heads * V_DIM), lambda i: (i, 0)),
        ),
        compiler_params=_cparams(("arbitrary",)),
        name="mla_qkv",
    )(a, q_a_gain, kv_a_gain, wq_ext, wkv, gq, gk, rope_c, rope_s)


def _nt_dot(a, b):
    return lax.dot_general(a, b, (((1,), (1,)), ((), ())), preferred_element_type=F32)


def _mla_attn_kernel(*refs, tk, n_lat):
    if n_lat:
        q_ref, kc_ref, vc_ref, kl_ref, vl_ref, o_ref = refs
    else:
        q_ref, kc_ref, vc_ref, o_ref = refs
    tq = q_ref.shape[0]

    def step(k, v, m, l, acc):
        s = _nt_dot(q_ref[...], k)
        m_new = jnp.maximum(m, jnp.max(s, axis=-1, keepdims=True))
        p = jnp.exp2(s - m_new)
        alpha = jnp.exp2(m - m_new)
        l_new = alpha * l + jnp.sum(p, axis=-1, keepdims=True)
        acc_new = alpha * acc + jnp.dot(p.astype(BF16), v, preferred_element_type=F32)
        return m_new, l_new, acc_new

    m0 = jnp.full((tq, 1), NEG_BIG, F32)
    l0 = jnp.zeros((tq, 1), F32)
    acc0 = jnp.zeros((tq, V_DIM), F32)
    carry = step(kc_ref[...], vc_ref[...], m0, l0, acc0)
    if n_lat:
        def body(i, carry):
            start = pl.multiple_of(i * tk, tk)
            return step(kl_ref[pl.ds(start, tk), :], vl_ref[pl.ds(start, tk), :], *carry)

        carry = lax.fori_loop(0, n_lat // tk, body, carry, unroll=2)
    _, l, acc = carry
    o_ref[...] = (acc * (1.0 / l)).astype(o_ref.dtype)


def mla_attention(q, k_ctx, v_ctx, k_lat=None, v_lat=None, *, tq, tk=512, heads=MLA_HEADS):
    nq = q.shape[0]
    n_ctx = k_ctx.shape[0]
    n_lat = 0 if k_lat is None else k_lat.shape[0]
    in_specs = [
        pl.BlockSpec((tq, MLA_QK_PAD), lambda h, i: (i, h)),
        pl.BlockSpec((n_ctx, MLA_QK_PAD), lambda h, i: (0, h)),
        pl.BlockSpec((n_ctx, V_DIM), lambda h, i: (0, h)),
    ]
    args = [q, k_ctx, v_ctx]
    if n_lat:
        in_specs += [
            pl.BlockSpec((n_lat, MLA_QK_PAD), lambda h, i: (0, h)),
            pl.BlockSpec((n_lat, V_DIM), lambda h, i: (0, h)),
        ]
        args += [k_lat, v_lat]
    return pl.pallas_call(
        functools.partial(_mla_attn_kernel, tk=tk, n_lat=n_lat),
        out_shape=jax.ShapeDtypeStruct((nq, heads * V_DIM), BF16),
        grid=(heads, nq // tq),
        in_specs=in_specs,
        out_specs=pl.BlockSpec((tq, V_DIM), lambda h, i: (i, h)),
        compiler_params=_cparams(("arbitrary", "arbitrary")),
        name="mla_attention",
    )(*args)


def _na_build_bias(rpb_ref, bias_sc, *, rb, span_rows, rows):
    nblk = rows // rb
    qc = lax.broadcasted_iota(jnp.int32, (GRID_W, LANES), 0)
    lane = lax.broadcasted_iota(jnp.int32, (GRID_W, LANES), 1)
    kc = lane & (GRID_W - 1)
    col_start = jnp.clip(qc - NA_KW // 2, 0, GRID_W - NA_KW)
    col_ok = (kc >= col_start) & (kc < col_start + NA_KW)
    first = lane < GRID_W
    neg = jnp.full((GRID_W, LANES), NEG_BIG, F32)

    def toeplitz(dr, shift):
        w = jnp.broadcast_to(rpb_ref[0, dr + NA_KH - 1:dr + NA_KH, :], (GRID_W, LANES))
        return pltpu.roll(w, shift, 1, stride=1, stride_axis=0)

    pairs = {}

    def pair(dr, ok0, ok1):
        key = (dr, ok0, ok1)
        if key not in pairs:
            a = toeplitz(dr, LANES - (NA_KW - 1)) if ok0 else neg
            b = toeplitz(dr + 1, GRID_W - (NA_KW - 1)) if ok1 else neg
            pairs[key] = jnp.where(col_ok, jnp.where(first, a, b), NEG_BIG)
        return pairs[key]

    for v, blk in enumerate((0, 1, nblk - 1)):
        kr0 = min(max(blk * rb - NA_KH // 2, 0), rows - span_rows)
        for qr in range(rb):
            r = blk * rb + qr
            r0 = min(max(r - NA_KH // 2, 0), rows - NA_KH)
            for m in range(span_rows // 2):
                ka = kr0 + 2 * m
                ok0 = r0 <= ka < r0 + NA_KH
                ok1 = r0 <= ka + 1 < r0 + NA_KH
                blk_val = pair(ka - r, ok0, ok1) if (ok0 or ok1) else neg
                bias_sc[v, qr * GRID_W:(qr + 1) * GRID_W, m * LANES:(m + 1) * LANES] = blk_val


def _na_attn_kernel(q_ref, k_ref, v_ref, kc_ref, vc_ref, rpb_ref, o_ref, bias_sc, *, rb, span_rows, rows):
    i = pl.program_id(1)
    nblk = rows // rb

    @pl.when(i == 0)
    def _():
        _na_build_bias(rpb_ref, bias_sc, rb=rb, span_rows=span_rows, rows=rows)

    kr0 = jnp.clip(i * rb - NA_KH // 2, 0, rows - span_rows)
    start = pl.multiple_of(kr0 * GRID_W, (NA_KH // 2) * GRID_W)
    span = span_rows * GRID_W
    variant = jnp.where(i == 0, 0, jnp.where(i == nblk - 1, 2, 1))
    q = q_ref[...]
    kw = k_ref[pl.ds(start, span), :]
    vw = v_ref[pl.ds(start, span), :]
    s_win = _nt_dot(q, kw) + bias_sc[variant]
    s_ctx = _nt_dot(q, kc_ref[...])
    m = jnp.maximum(jnp.max(s_win, axis=-1, keepdims=True), jnp.max(s_ctx, axis=-1, keepdims=True))
    p_win = jnp.exp(s_win - m)
    p_ctx = jnp.exp(s_ctx - m)
    l = jnp.sum(p_win, axis=-1, keepdims=True) + jnp.sum(p_ctx, axis=-1, keepdims=True)
    o = (jnp.dot(p_win.astype(BF16), vw, preferred_element_type=F32)
         + jnp.dot(p_ctx.astype(BF16), vc_ref[...], preferred_element_type=F32))
    o_ref[...] = (o * (1.0 / l)).astype(o_ref.dtype)


def na_attention(qkv, qkv_ctx, rpb, *, rb=8, heads=NA_HEADS):
    n = qkv.shape[0]
    n_ctx = qkv_ctx.shape[0]
    rows = n // GRID_W
    span_rows = rb + NA_KH
    nblk = rows // rb
    tq = rb * GRID_W
    rpb_pad = jnp.pad(rpb, ((0, 0), (0, 2 * NA_KH - rpb.shape[1]), (0, LANES - rpb.shape[2])))
    return pl.pallas_call(
        functools.partial(_na_attn_kernel, rb=rb, span_rows=span_rows, rows=rows),
        out_shape=jax.ShapeDtypeStruct((n, heads * NA_HEAD_DIM), BF16),
        grid=(heads, nblk),
        in_specs=[
            pl.BlockSpec((tq, NA_HEAD_DIM), lambda h, i: (i, h)),
            pl.BlockSpec((n, NA_HEAD_DIM), lambda h, i: (0, heads + h)),
            pl.BlockSpec((n, NA_HEAD_DIM), lambda h, i: (0, 2 * heads + h)),
            pl.BlockSpec((n_ctx, NA_HEAD_DIM), lambda h, i: (0, heads + h)),
            pl.BlockSpec((n_ctx, NA_HEAD_DIM), lambda h, i: (0, 2 * heads + h)),
            pl.BlockSpec((1, 2 * NA_KH, LANES), lambda h, i: (h, 0, 0)),
        ],
        out_specs=pl.BlockSpec((tq, NA_HEAD_DIM), lambda h, i: (i, h)),
        scratch_shapes=[pltpu.VMEM((3, tq, span_rows * GRID_W), F32)],
        compiler_params=_cparams(("arbitrary", "arbitrary")),
        name="na_attention",
    )(qkv, qkv, qkv, qkv_ctx, qkv_ctx, rpb_pad)


def _outproj_kernel(o_ref, w_ref, x_ref, ga_ref, g_ref, sc_ref, sh_ref, rwh_ref, rwl_ref,
                    xo_ref, f_ref, aff_ref, *, n_experts):
    acc = jnp.dot(o_ref[...], w_ref[...], preferred_element_type=F32)
    xn = x_ref[...] + ga_ref[...] * acc
    xo_ref[...] = xn
    y = xn * lax.rsqrt(jnp.mean(xn * xn, axis=-1, keepdims=True) + EPS) * g_ref[...]
    f = y * (1.0 + sc_ref[...]) + sh_ref[...]
    f_ref[...] = f
    f_hi = f.astype(BF16)
    f_lo = (f - f_hi.astype(F32)).astype(BF16)
    logits = (jnp.dot(f_hi, rwh_ref[...], preferred_element_type=F32)
              + jnp.dot(f_lo, rwh_ref[...], preferred_element_type=F32)
              + jnp.dot(f_hi, rwl_ref[...], preferred_element_type=F32))
    lane = lax.broadcasted_iota(jnp.int32, logits.shape, 1)
    logits = jnp.where(lane < n_experts, logits, NEG_BIG)
    e = jnp.exp(logits - jnp.max(logits, axis=-1, keepdims=True))
    aff_ref[...] = e * (1.0 / jnp.sum(e, axis=-1, keepdims=True))


def outproj_ffnprep(o, w_o, x, gate_a, gain, scale, shift, rw_hi, rw_lo, *, tm, n_experts=N_EXPERTS):
    m, d = x.shape
    k = o.shape[1]
    row = lambda i: (i, 0)
    full = lambda i: (0, 0)
    return pl.pallas_call(
        functools.partial(_outproj_kernel, n_experts=n_experts),
        out_shape=(
            jax.ShapeDtypeStruct((m, d), F32),
            jax.ShapeDtypeStruct((m, d), F32),
            jax.ShapeDtypeStruct((m, LANES), F32),
        ),
        grid=(m // tm,),
        in_specs=[
            pl.BlockSpec((tm, k), row),
            pl.BlockSpec((k, d), full),
            pl.BlockSpec((tm, d), row),
            pl.BlockSpec((1, d), full),
            pl.BlockSpec((1, d), full),
            pl.BlockSpec((1, d), full),
            pl.BlockSpec((1, d), full),
            pl.BlockSpec((d, LANES), full),
            pl.BlockSpec((d, LANES), full),
        ],
        out_specs=(
            pl.BlockSpec((tm, d), row),
            pl.BlockSpec((tm, d), row),
            pl.BlockSpec((tm, LANES), row),
        ),
        compiler_params=_cparams(("arbitrary",)),
        name="outproj_ffnprep",
    )(o, w_o, x, gate_a, gain, scale, shift, rw_hi, rw_lo)


def _prefix_rank(sel_bf, upper_incl, lower_strict):
    w = jnp.dot(sel_bf, upper_incl, preferred_element_type=F32)
    row_tot = jnp.broadcast_to(w[:, LANES - 1:LANES], w.shape)
    row_off = jnp.dot(lower_strict, row_tot.astype(BF16), preferred_element_type=F32)
    return w, row_tot, row_off


def _topk_kernel(a_ref, tok_ref, gate_ref, *, cap, jc):
    a = a_ref[0]
    tb = a.shape[0]
    ri = lax.broadcasted_iota(jnp.int32, (tb, LANES), 0)
    ci = lax.broadcasted_iota(jnp.int32, (tb, LANES), 1)
    upper_incl = (ri <= ci).astype(BF16)
    lower_strict = (ci < ri).astype(BF16)

    def count(mask):
        c = jnp.sum(mask.astype(F32), axis=1, keepdims=True)
        return jnp.sum(c, axis=0, keepdims=True)

    thr = jnp.zeros((1, 1), jnp.int32)
    for bit in range(30, -1, -1):
        cand = thr | jnp.int32(1 << bit)
        candf = lax.bitcast_convert_type(cand, F32)
        thr = jnp.where(count(a >= candf) >= cap, cand, thr)
    thrf = lax.bitcast_convert_type(thr, F32)
    gt = a > thrf
    eq = a == thrf
    need = cap - count(gt)
    w_eq, _, off_eq = _prefix_rank(eq.astype(BF16), upper_incl, lower_strict)
    sel = gt | (eq & (w_eq + off_eq <= need))
    sel_bf = sel.astype(BF16)
    w, row_tot, row_off = _prefix_rank(sel_bf, upper_incl, lower_strict)
    tot_t = row_tot.T
    incl_t = jnp.dot(tot_t.astype(BF16), upper_incl, preferred_element_type=F32)
    excl_t = incl_t - tot_t
    excl_row = excl_t[0:1, :]
    incl_row = incl_t[0:1, :]
    w_bf = w.astype(BF16)
    a1 = a.astype(BF16)
    a2 = (a - a1.astype(F32)).astype(BF16)
    a3 = (a - a1.astype(F32) - a2.astype(F32)).astype(BF16)
    lane_f = lax.broadcasted_iota(jnp.int32, (jc, LANES), 1).astype(F32)
    for c in range(cap // jc):
        slot = (lax.broadcasted_iota(jnp.int32, (jc, 1), 0) + c * jc).astype(F32)
        in_row = (excl_row <= slot) & (slot < incl_row)
        in_row_f = in_row.astype(F32)
        in_row_bf = in_row.astype(BF16)
        w_g = jnp.dot(in_row_bf, w_bf, preferred_element_type=F32)
        s_g = jnp.dot(in_row_bf, sel_bf, preferred_element_type=F32)
        off_g = jnp.sum(in_row_f * excl_row, axis=1, keepdims=True)
        hit = (s_g > 0.5) & (w_g + off_g == slot + 1.0)
        hit_f = hit.astype(F32)
        row_id = jnp.sum(in_row_f * lane_f, axis=1, keepdims=True)
        lane_id = jnp.sum(hit_f * lane_f, axis=1, keepdims=True)
        a_g = (jnp.dot(in_row_bf, a1, preferred_element_type=F32)
               + jnp.dot(in_row_bf, a2, preferred_element_type=F32)
               + jnp.dot(in_row_bf, a3, preferred_element_type=F32))
        tok_ref[0, c * jc:(c + 1) * jc, :] = (row_id * float(LANES) + lane_id).astype(jnp.int32)
        gate_ref[0, c * jc:(c + 1) * jc, :] = jnp.sum(hit_f * a_g, axis=1, keepdims=True)


def expert_choice(aff_t, *, cap):
    e, t = aff_t.shape
    t_pad = LANES * LANES
    a = jnp.pad(aff_t, ((0, 0), (0, t_pad - t)), constant_values=-1.0).reshape(e, LANES, LANES)
    jc = min(cap, 256)
    tok, gate = pl.pallas_call(
        functools.partial(_topk_kernel, cap=cap, jc=jc),
        out_shape=(
            jax.ShapeDtypeStruct((e, cap, 1), jnp.int32),
            jax.ShapeDtypeStruct((e, cap, 1), F32),
        ),
        grid=(e,),
        in_specs=[pl.BlockSpec((1, LANES, LANES), lambda i: (i, 0, 0))],
        out_specs=(
            pl.BlockSpec((1, cap, 1), lambda i: (i, 0, 0)),
            pl.BlockSpec((1, cap, 1), lambda i: (i, 0, 0)),
        ),
        compiler_params=_cparams(("arbitrary",)),
        name="expert_choice",
    )(a)
    return tok.reshape(e, cap), gate


DMA_UNROLL = 8


def _moe_kernel(idx_hbm, f_hbm, x_hbm, g_ref, gf_ref, wg_ref, wu_ref, wd_ref, o_hbm,
                idx_s, xs, xo, sems, *, tm, chunks):
    del x_hbm
    step = pl.program_id(0) * chunks + pl.program_id(1)
    idx_copy = pltpu.make_async_copy(idx_hbm.at[step], idx_s, sems.at[0])
    idx_copy.start()
    idx_copy.wait()

    def f_row(j):
        return pltpu.make_async_copy(f_hbm.at[pl.ds(idx_s[j], 1)], xs.at[pl.ds(j, 1)], sems.at[1])

    def x_row_in(j):
        return pltpu.make_async_copy(o_hbm.at[pl.ds(idx_s[j], 1)], xo.at[pl.ds(j, 1)], sems.at[2])

    def x_row_out(j):
        return pltpu.make_async_copy(xo.at[pl.ds(j, 1)], o_hbm.at[pl.ds(idx_s[j], 1)], sems.at[3])

    def start_gather(j, c):
        f_row(j).start()
        x_row_in(j).start()
        return c

    def wait_gather(j, c):
        f_row(j).wait()
        x_row_in(j).wait()
        return c

    lax.fori_loop(0, tm, start_gather, 0, unroll=DMA_UNROLL)
    lax.fori_loop(0, tm, wait_gather, 0, unroll=DMA_UNROLL)
    xb = xs[...].astype(BF16)
    hg = jnp.dot(xb, wg_ref[0], preferred_element_type=F32)
    hu = jnp.dot(xb, wu_ref[0], preferred_element_type=F32)
    hid = (hg * (1.0 / (1.0 + jnp.exp(-hg)))) * hu
    y = jnp.dot(hid.astype(BF16), wd_ref[0], preferred_element_type=F32)
    xo[...] = xo[...] + gf_ref[...] * (y * g_ref[...])

    def start_scatter(j, c):
        x_row_out(j).start()
        return c

    def wait_scatter(j, c):
        x_row_out(j).wait()
        return c

    lax.fori_loop(0, tm, start_scatter, 0, unroll=DMA_UNROLL)
    lax.fori_loop(0, tm, wait_scatter, 0, unroll=DMA_UNROLL)


def moe_apply(f, x, tok, gate, gate_f, wg, wu, wd, *, tm):
    t, d = x.shape
    e, cap = tok.shape
    chunks = cap // tm
    ff = wg.shape[2]
    return pl.pallas_call(
        functools.partial(_moe_kernel, tm=tm, chunks=chunks),
        out_shape=jax.ShapeDtypeStruct((t, d), F32),
        grid=(e, chunks),
        in_specs=[
            pl.BlockSpec(memory_space=pl.ANY),
            pl.BlockSpec(memory_space=pl.ANY),
            pl.BlockSpec(memory_space=pl.ANY),
            pl.BlockSpec((tm, 1), lambda i, c: (i * chunks + c, 0)),
            pl.BlockSpec((1, d), lambda i, c: (0, 0)),
            pl.BlockSpec((1, d, ff), lambda i, c: (i, 0, 0)),
            pl.BlockSpec((1, d, ff), lambda i, c: (i, 0, 0)),
            pl.BlockSpec((1, ff, d), lambda i, c: (i, 0, 0)),
        ],
        out_specs=pl.BlockSpec(memory_space=pl.ANY),
        scratch_shapes=[
            pltpu.SMEM((tm,), jnp.int32),
            pltpu.VMEM((tm, d), F32),
            pltpu.VMEM((tm, d), F32),
            pltpu.SemaphoreType.DMA((4,)),
        ],
        input_output_aliases={2: 0},
        compiler_params=_cparams(("arbitrary", "arbitrary")),
        name="moe_apply",
    )(tok.reshape(e * chunks, tm), f, x, gate.reshape(e * cap, 1), gate_f, wg, wu, wd)


def _swap_rot_halves(w):
    q = QK_ROPE // 4
    return jnp.concatenate([w[..., q:2 * q], w[..., :q], w[..., 3 * q:], w[..., 2 * q:3 * q]], axis=-1)


def _pad_lanes(w):
    return jnp.pad(w, [(0, 0)] * (w.ndim - 1) + [(0, LANES - w.shape[-1])])


def _rope_tables(n):
    half = QK_ROPE // 2
    inv = ROPE_THETA ** (-jnp.arange(0, half, 2, dtype=F32) / half)
    t = jnp.arange(n)
    ang_r = (t // GRID_W).astype(F32)[:, None] * inv[None, :]
    ang_c = (t % GRID_W).astype(F32)[:, None] * inv[None, :]
    cr, sr, cc, sc = jnp.cos(ang_r), jnp.sin(ang_r), jnp.cos(ang_c), jnp.sin(ang_c)
    return (_pad_lanes(jnp.concatenate([cr, cr, cc, cc], axis=-1)),
            _pad_lanes(jnp.concatenate([-sr, sr, -sc, sc], axis=-1)))


def _gain_rows(g):
    pe = g[QK_NOPE:]
    rows = jnp.stack([g[:QK_NOPE], _pad_lanes(pe), _pad_lanes(_swap_rot_halves(pe))])
    return jnp.pad(rows, ((0, 5), (0, 0)))


def _moe_layer(f, x, aff, gate_f, wg, wu, wd, *, tm):
    t = x.shape[0]
    cap = EC_FACTOR * t // N_EXPERTS
    tok, gate = expert_choice(aff[:, :N_EXPERTS].T, cap=cap)
    return moe_apply(f, x, tok, gate, gate_f, wg, wu, wd, tm=min(tm, cap))


def kernel(x, c, ctx, c_ctx, ada_w, ada_b, norm_mix, norm_ffn, mla_w_in, mla_q_a_gain, mla_kv_a_gain,
           mla_w_qb, mla_w_kvb, mla_q_gain, mla_k_gain, mla_w_o, na_w_qkv, na_q_gain, na_k_gain, na_rpb,
           na_w_o, router_w, moe_w_gate, moe_w_up, moe_w_down):
    n = x.shape[1]
    n_ctx = ctx.shape[1]
    d = D_MODEL
    x_lat = x[0]
    x_ctx = ctx[0]

    mods = ada_modulation(jnp.stack([c[0], c_ctx]), ada_w, ada_b)

    def mod(layer, r, k):
        return mods[layer, r:r + 1, k * d:(k + 1) * d]

    row = lambda v: v.reshape(1, -1)
    rw_hi = [_pad_lanes(router_w[i]).astype(BF16) for i in range(2)]
    rw_lo = [(_pad_lanes(router_w[i]) - rw_hi[i].astype(F32)).astype(BF16) for i in range(2)]
    wg = moe_w_gate.astype(BF16)
    wu = moe_w_up.astype(BF16)
    wd = moe_w_down.astype(BF16)

    w_in = mla_w_in[0]
    kpe_w = w_in[:, Q_LORA + KV_LORA:]
    w_in_ext = jnp.concatenate(
        [w_in[:, :Q_LORA + KV_LORA], _pad_lanes(kpe_w), _pad_lanes(_swap_rot_halves(kpe_w))], axis=1).astype(BF16)
    wq = mla_w_qb[0].reshape(Q_LORA, MLA_HEADS, QK_HEAD)
    wq_ext = jnp.concatenate(
        [wq[..., :QK_NOPE], _pad_lanes(wq[..., QK_NOPE:]), _pad_lanes(_swap_rot_halves(wq[..., QK_NOPE:]))],
        axis=-1).reshape(Q_LORA, -1).astype(BF16)
    wkv = mla_w_kvb[0].astype(BF16)
    gq = _gain_rows(mla_q_gain[0])
    gk = _gain_rows(mla_k_gain[0])
    rope_c, rope_s = _rope_tables(n)
    ones_c = _pad_lanes(jnp.ones((n_ctx, QK_ROPE), F32))
    zeros_s = jnp.zeros((n_ctx, LANES), F32)
    no_gain = jnp.ones((1, w_in_ext.shape[1]), F32)

    a_lat = norm_mod_proj(x_lat, row(norm_mix[0]), mod(0, 0, 1), mod(0, 0, 0), w_in_ext, no_gain,
                          tm=512, tn=w_in_ext.shape[1], out_dtype=F32)
    a_ctx = norm_mod_proj(x_ctx, row(norm_mix[0]), mod(0, 1, 1), mod(0, 1, 0), w_in_ext, no_gain,
                          tm=n_ctx, tn=w_in_ext.shape[1], out_dtype=F32)
    q_l, k_l, v_l = mla_qkv(a_lat, row(mla_q_a_gain[0]), row(mla_kv_a_gain[0]), wq_ext, wkv, gq, gk,
                            rope_c, rope_s, tm=256)
    q_c, k_c, v_c = mla_qkv(a_ctx, row(mla_q_a_gain[0]), row(mla_kv_a_gain[0]), wq_ext, wkv, gq, gk,
                            ones_c, zeros_s, tm=n_ctx)
    o_lat = mla_attention(q_l, k_c, v_c, k_l, v_l, tq=1024)
    o_ctx = mla_attention(q_c, k_c, v_c, tq=n_ctx)

    w_o = mla_w_o[0].astype(BF16)
    x_lat, f_lat, aff_lat = outproj_ffnprep(o_lat, w_o, x_lat, mod(0, 0, 2), row(norm_ffn[0]), mod(0, 0, 4),
                                            mod(0, 0, 3), rw_hi[0], rw_lo[0], tm=512)
    x_ctx, f_ctx, aff_ctx = outproj_ffnprep(o_ctx, w_o, x_ctx, mod(0, 1, 2), row(norm_ffn[0]), mod(0, 1, 4),
                                            mod(0, 1, 3), rw_hi[0], rw_lo[0], tm=n_ctx)
    x_lat = _moe_layer(f_lat, x_lat, aff_lat, mod(0, 0, 5), wg[0], wu[0], wd[0], tm=256)
    x_ctx = _moe_layer(f_ctx, x_ctx, aff_ctx, mod(0, 1, 5), wg[0], wu[0], wd[0], tm=256)

    w_qkv = na_w_qkv[0].astype(BF16)
    qkv_gain = jnp.concatenate([jnp.tile(na_q_gain[0], NA_HEADS) * NA_HEAD_DIM ** -0.5,
                                jnp.tile(na_k_gain[0], NA_HEADS), jnp.ones((d,), F32)]).reshape(1, -1)
    qkv_l = norm_mod_proj(x_lat, row(norm_mix[1]), mod(1, 0, 1), mod(1, 0, 0), w_qkv, qkv_gain,
                          tm=512, tn=1024, out_dtype=BF16, n_norm_blocks=4)
    qkv_c = norm_mod_proj(x_ctx, row(norm_mix[1]), mod(1, 1, 1), mod(1, 1, 0), w_qkv, qkv_gain,
                          tm=n_ctx, tn=1024, out_dtype=BF16, n_norm_blocks=4)
    o_lat = na_attention(qkv_l, qkv_c, na_rpb[0])
    x_lat, f_lat, aff_lat = outproj_ffnprep(o_lat, na_w_o[0].astype(BF16), x_lat, mod(1, 0, 2), row(norm_ffn[1]),
                                            mod(1, 0, 4), mod(1, 0, 3), rw_hi[1], rw_lo[1], tm=512)
    x_lat = _moe_layer(f_lat, x_lat, aff_lat, mod(1, 0, 5), wg[1], wu[1], wd[1], tm=256)
    return x_lat[None]
```

```python
import functools
import math

import jax
import jax.numpy as jnp
from jax import lax
from jax.experimental import pallas as pl
from jax.experimental.pallas import tpu as pltpu

F32 = jnp.float32
BF16 = jnp.bfloat16

D_MODEL = 2048
GRID_W = 64
EPS = 1e-6
ROPE_THETA = 10000.0

MLA_HEADS = 16
Q_LORA = 512
KV_LORA = 512
QK_NOPE = 128
QK_ROPE = 64
QK_HEAD = QK_NOPE + QK_ROPE
V_DIM = 128
MLA_QK_PAD = 256

NA_HEADS = 16
NA_HEAD_DIM = 128
NA_KH = 8
NA_KW = 16

N_EXPERTS = 16
EXPERT_FF = 1408
EC_FACTOR = 2

LANES = 128
SUBLANES = 8
NEG_BIG = -1e30

VMEM_LIMIT = 56 * 1024 * 1024


def _cparams(sem):
    return pltpu.CompilerParams(dimension_semantics=sem, vmem_limit_bytes=VMEM_LIMIT)


def _ada_kernel(c_ref, w_ref, b_ref, o_ref, *, tn):
    for r in range(2):
        c = c_ref[r]
        act = c * (1.0 / (1.0 + jnp.exp(-c)))
        for k in range(tn // LANES):
            sl = slice(k * LANES, (k + 1) * LANES)
            o_ref[0, r:r + 1, sl] = jnp.sum(act * w_ref[0, :, sl], axis=0, keepdims=True) + b_ref[0, :, sl]


def ada_modulation(cvec, ada_w, ada_b, *, tn=512):
    depth, d, n = ada_w.shape
    c_rep = jnp.broadcast_to(cvec[:, :, None], (2, d, LANES))
    return pl.pallas_call(
        functools.partial(_ada_kernel, tn=tn),
        out_shape=jax.ShapeDtypeStruct((depth, 2, n), F32),
        grid=(depth, n // tn),
        in_specs=[
            pl.BlockSpec((2, d, LANES), lambda l, j: (0, 0, 0)),
            pl.BlockSpec((1, d, tn), lambda l, j: (l, 0, j)),
            pl.BlockSpec((1, 1, tn), lambda l, j: (l, 0, j)),
        ],
        out_specs=pl.BlockSpec((1, 2, tn), lambda l, j: (l, 0, j)),
        compiler_params=_cparams(("arbitrary", "arbitrary")),
        name="ada_modulation",
    )(c_rep, ada_w, ada_b.reshape(depth, 1, n))


def _proj_kernel(x_ref, g_ref, sc_ref, sh_ref, w_ref, cg_ref, o_ref, h_ref, *, tn, n_norm_blocks, head):
    j = pl.program_id(1)

    @pl.when(j == 0)
    def _():
        x = x_ref[...]
        y = x * lax.rsqrt(jnp.mean(x * x, axis=-1, keepdims=True) + EPS) * g_ref[...]
        h_ref[...] = (y * (1.0 + sc_ref[...]) + sh_ref[...]).astype(BF16)

    acc = jnp.dot(h_ref[...], w_ref[...], preferred_element_type=F32)
    if n_norm_blocks == 0:
        o_ref[...] = acc.astype(o_ref.dtype)
    else:
        @pl.when(j < n_norm_blocks)
        def _():
            for c in range(tn // head):
                sl = slice(c * head, (c + 1) * head)
                y = acc[:, sl]
                r = lax.rsqrt(jnp.mean(y * y, axis=-1, keepdims=True) + EPS)
                o_ref[:, sl] = (y * r * cg_ref[:, sl]).astype(o_ref.dtype)

        @pl.when(j >= n_norm_blocks)
        def _():
            o_ref[...] = acc.astype(o_ref.dtype)


def norm_mod_proj(x, gain, scale, shift, w, col_gain, *, tm, tn, out_dtype, n_norm_blocks=0, head=LANES):
    m, d = x.shape
    n = w.shape[1]
    return pl.pallas_call(
        functools.partial(_proj_kernel, tn=tn, n_norm_blocks=n_norm_blocks, head=head),
        out_shape=jax.ShapeDtypeStruct((m, n), out_dtype),
        grid=(m // tm, n // tn),
        in_specs=[
            pl.BlockSpec((tm, d), lambda i, j: (i, 0)),
            pl.BlockSpec((1, d), lambda i, j: (0, 0)),
            pl.BlockSpec((1, d), lambda i, j: (0, 0)),
            pl.BlockSpec((1, d), lambda i, j: (0, 0)),
            pl.BlockSpec((d, tn), lambda i, j: (0, j)),
            pl.BlockSpec((1, tn), lambda i, j: (0, j)),
        ],
        out_specs=pl.BlockSpec((tm, tn), lambda i, j: (i, j)),
        scratch_shapes=[pltpu.VMEM((tm, d), BF16)],
        compiler_params=_cparams(("arbitrary", "arbitrary")),
        name="norm_mod_proj",
    )(x, gain, scale, shift, w, col_gain)


def _mla_qkv_kernel(a_ref, qag_ref, kvag_ref, wq_ref, wkv_ref, gq_ref, gk_ref, rc_ref, rs_ref,
                    q_ref, k_ref, vt_ref, *, heads, q_scale):
    a = a_ref[...]
    qc = a[:, :Q_LORA]
    kvc = a[:, Q_LORA:Q_LORA + KV_LORA]
    kpe1 = a[:, Q_LORA + KV_LORA:Q_LORA + KV_LORA + LANES]
    kpe2 = a[:, Q_LORA + KV_LORA + LANES:]

    def rms(x, g):
        return x * lax.rsqrt(jnp.mean(x * x, axis=-1, keepdims=True) + EPS) * g

    qn = rms(qc, qag_ref[...]).astype(BF16)
    kvn = rms(kvc, kvag_ref[...]).astype(BF16)
    rc = rc_ref[...]
    rs = rs_ref[...]
    gq_nope, gq1, gq2 = gq_ref[0:1, :], gq_ref[1:2, :], gq_ref[2:3, :]
    gk_nope, gk1, gk2 = gk_ref[0:1, :], gk_ref[1:2, :], gk_ref[2:3, :]
    kpe_ss = jnp.sum(kpe1 * kpe1, axis=-1, keepdims=True)
    kpe_rot = kpe1 * gk1 * rc + kpe2 * gk2 * rs
    qw = QK_NOPE + 2 * LANES
    for h in range(heads):
        qh = jnp.dot(qn, wq_ref[:, h * qw:(h + 1) * qw], preferred_element_type=F32)
        qa, q1, q2 = qh[:, :QK_NOPE], qh[:, QK_NOPE:QK_NOPE + LANES], qh[:, QK_NOPE + LANES:]
        ss = jnp.sum(qa * qa, axis=-1, keepdims=True) + jnp.sum(q1 * q1, axis=-1, keepdims=True)
        r = lax.rsqrt(ss * (1.0 / QK_HEAD) + EPS) * q_scale
        q_ref[:, h * MLA_QK_PAD:h * MLA_QK_PAD + QK_NOPE] = (qa * r * gq_nope).astype(BF16)
        q_ref[:, h * MLA_QK_PAD + QK_NOPE:(h + 1) * MLA_QK_PAD] = (
            (q1 * gq1 * rc + q2 * gq2 * rs) * r).astype(BF16)
        kvh = jnp.dot(kvn, wkv_ref[:, h * (QK_NOPE + V_DIM):(h + 1) * (QK_NOPE + V_DIM)],
                      preferred_element_type=F32)
        kn, vv = kvh[:, :QK_NOPE], kvh[:, QK_NOPE:]
        rk = lax.rsqrt((jnp.sum(kn * kn, axis=-1, keepdims=True) + kpe_ss) * (1.0 / QK_HEAD) + EPS)
        k_ref[:, h * MLA_QK_PAD:h * MLA_QK_PAD + QK_NOPE] = (kn * rk * gk_nope).astype(BF16)
        k_ref[:, h * MLA_QK_PAD + QK_NOPE:(h + 1) * MLA_QK_PAD] = (kpe_rot * rk).astype(BF16)
        vt_ref[h * V_DIM:(h + 1) * V_DIM, :] = vv.T.astype(BF16)


def mla_qkv(a, q_a_gain, kv_a_gain, wq_ext, wkv, gq, gk, rope_c, rope_s, *, tm, heads=MLA_HEADS):
    m, aw = a.shape
    full = lambda i: (0, 0)
    return pl.pallas_call(
        functools.partial(_mla_qkv_kernel, heads=heads, q_scale=QK_HEAD ** -0.5 * math.log2(math.e)),
        out_shape=(
            jax.ShapeDtypeStruct((m, heads * MLA_QK_PAD), BF16),
            jax.ShapeDtypeStruct((m, heads * MLA_QK_PAD), BF16),
            jax.ShapeDtypeStruct((heads * V_DIM, m), BF16),
        ),
        grid=(m // tm,),
        in_specs=[
            pl.BlockSpec((tm, aw), lambda i: (i, 0)),
            pl.BlockSpec((1, Q_LORA), full),
            pl.BlockSpec((1, KV_LORA), full),
            pl.BlockSpec(wq_ext.shape, full),
            pl.BlockSpec(wkv.shape, full),
            pl.BlockSpec((8, LANES), full),
            pl.BlockSpec((8, LANES), full),
            pl.BlockSpec((tm, LANES), lambda i: (i, 0)),
            pl.BlockSpec((tm, LANES), lambda i: (i, 0)),
        ],
        out_specs=(
            pl.BlockSpec((tm, heads * MLA_QK_PAD), lambda i: (i, 0)),
            pl.BlockSpec((tm, heads * MLA_QK_PAD), lambda i: (i, 0)),
            pl.BlockSpec((heads * V_DIM, tm), lambda i: (0, i)),
        ),
        compiler_params=_cparams(("arbitrary",)),
        name="mla_qkv",
    )(a, q_a_gain, kv_a_gain, wq_ext, wkv, gq, gk, rope_c, rope_s)


def _nt_dot(a, b):
    return lax.dot_general(a, b, (((1,), (1,)), ((), ())), preferred_element_type=F32)


def _mla_attn_kernel(*refs, tk, n_lat):
    if n_lat:
        q_ref, kc_ref, vtc_ref, kl_ref, vtl_ref, o_ref = refs
    else:
        q_ref, kc_ref, vtc_ref, o_ref = refs
    tq = q_ref.shape[0]

    def step(k, vt, m, l, acc):
        s = _nt_dot(k, q_ref[...])
        m_new = jnp.maximum(m, jnp.max(s, axis=0, keepdims=True))
        p = jnp.exp2(s - m_new)
        alpha = jnp.exp2(m - m_new)
        l_new = alpha * l + jnp.sum(p, axis=0, keepdims=True)
        acc_new = alpha * acc + jnp.dot(vt, p.astype(BF16), preferred_element_type=F32)
        return m_new, l_new, acc_new

    m0 = jnp.full((1, tq), NEG_BIG, F32)
    l0 = jnp.zeros((1, tq), F32)
    acc0 = jnp.zeros((V_DIM, tq), F32)
    carry = step(kc_ref[...], vtc_ref[...], m0, l0, acc0)
    if n_lat:
        def body(i, carry):
            start = pl.multiple_of(i * tk, tk)
            return step(kl_ref[pl.ds(start, tk), :], vtl_ref[:, pl.ds(start, tk)], *carry)

        carry = lax.fori_loop(0, n_lat // tk, body, carry, unroll=2)
    _, l, acc = carry
    o_ref[...] = (acc * (1.0 / l)).T.astype(o_ref.dtype)


def mla_attention(q, k_ctx, vt_ctx, k_lat=None, vt_lat=None, *, tq, tk=512, heads=MLA_HEADS):
    nq = q.shape[0]
    n_ctx = k_ctx.shape[0]
    n_lat = 0 if k_lat is None else k_lat.shape[0]
    in_specs = [
        pl.BlockSpec((tq, MLA_QK_PAD), lambda h, i: (i, h)),
        pl.BlockSpec((n_ctx, MLA_QK_PAD), lambda h, i: (0, h)),
        pl.BlockSpec((V_DIM, n_ctx), lambda h, i: (h, 0)),
    ]
    args = [q, k_ctx, vt_ctx]
    if n_lat:
        in_specs += [
            pl.BlockSpec((n_lat, MLA_QK_PAD), lambda h, i: (0, h)),
            pl.BlockSpec((V_DIM, n_lat), lambda h, i: (h, 0)),
        ]
        args += [k_lat, vt_lat]
    return pl.pallas_call(
        functools.partial(_mla_attn_kernel, tk=tk, n_lat=n_lat),
        out_shape=jax.ShapeDtypeStruct((nq, heads * V_DIM), BF16),
        grid=(heads, nq // tq),
        in_specs=in_specs,
        out_specs=pl.BlockSpec((tq, V_DIM), lambda h, i: (i, h)),
        compiler_params=_cparams(("arbitrary", "arbitrary")),
        name="mla_attention",
    )(*args)


def _na_build_bias(rpb_ref, bias_sc, *, rb, span_rows, rows):
    nblk = rows // rb
    qc = lax.broadcasted_iota(jnp.int32, (GRID_W, LANES), 0)
    lane = lax.broadcasted_iota(jnp.int32, (GRID_W, LANES), 1)
    kc = lane & (GRID_W - 1)
    col_start = jnp.clip(qc - NA_KW // 2, 0, GRID_W - NA_KW)
    col_ok = (kc >= col_start) & (kc < col_start + NA_KW)
    first = lane < GRID_W
    neg = jnp.full((GRID_W, LANES), NEG_BIG, F32)

    def toeplitz(dr, shift):
        w = jnp.broadcast_to(rpb_ref[0, dr + NA_KH - 1:dr + NA_KH, :], (GRID_W, LANES))
        return pltpu.roll(w, shift, 1, stride=1, stride_axis=0)

    pairs = {}

    def pair(dr, ok0, ok1):
        key = (dr, ok0, ok1)
        if key not in pairs:
            a = toeplitz(dr, LANES - (NA_KW - 1)) if ok0 else neg
            b = toeplitz(dr + 1, GRID_W - (NA_KW - 1)) if ok1 else neg
            pairs[key] = jnp.where(col_ok, jnp.where(first, a, b), NEG_BIG)
        return pairs[key]

    for v, blk in enumerate((0, 1, nblk - 1)):
        kr0 = min(max(blk * rb - NA_KH // 2, 0), rows - span_rows)
        for qr in range(rb):
            r = blk * rb + qr
            r0 = min(max(r - NA_KH // 2, 0), rows - NA_KH)
            for m in range(span_rows // 2):
                ka = kr0 + 2 * m
                ok0 = r0 <= ka < r0 + NA_KH
                ok1 = r0 <= ka + 1 < r0 + NA_KH
                blk_val = pair(ka - r, ok0, ok1) if (ok0 or ok1) else neg
                bias_sc[v, qr * GRID_W:(qr + 1) * GRID_W, m * LANES:(m + 1) * LANES] = blk_val


def _na_attn_kernel(q_ref, k_ref, v_ref, kc_ref, vc_ref, rpb_ref, o_ref, bias_sc, *, rb, span_rows, rows):
    i = pl.program_id(1)
    nblk = rows // rb

    @pl.when(i == 0)
    def _():
        _na_build_bias(rpb_ref, bias_sc, rb=rb, span_rows=span_rows, rows=rows)

    kr0 = jnp.clip(i * rb - NA_KH // 2, 0, rows - span_rows)
    start = pl.multiple_of(kr0 * GRID_W, (NA_KH // 2) * GRID_W)
    span = span_rows * GRID_W
    variant = jnp.where(i == 0, 0, jnp.where(i == nblk - 1, 2, 1))
    q = q_ref[...]
    kw = k_ref[pl.ds(start, span), :]
    vw = v_ref[pl.ds(start, span), :]
    s_win = _nt_dot(q, kw) + bias_sc[variant]
    s_ctx = _nt_dot(q, kc_ref[...])
    m = jnp.maximum(jnp.max(s_win, axis=-1, keepdims=True), jnp.max(s_ctx, axis=-1, keepdims=True))
    p_win = jnp.exp(s_win - m)
    p_ctx = jnp.exp(s_ctx - m)
    l = jnp.sum(p_win, axis=-1, keepdims=True) + jnp.sum(p_ctx, axis=-1, keepdims=True)
    o = (jnp.dot(p_win.astype(BF16), vw, preferred_element_type=F32)
         + jnp.dot(p_ctx.astype(BF16), vc_ref[...], preferred_element_type=F32))
    o_ref[...] = (o * (1.0 / l)).astype(o_ref.dtype)


def na_attention(qkv, qkv_ctx, rpb, *, rb=8, heads=NA_HEADS):
    n = qkv.shape[0]
    n_ctx = qkv_ctx.shape[0]
    rows = n // GRID_W
    span_rows = rb + NA_KH
    nblk = rows // rb
    tq = rb * GRID_W
    rpb_pad = jnp.pad(rpb, ((0, 0), (0, 2 * NA_KH - rpb.shape[1]), (0, LANES - rpb.shape[2])))
    return pl.pallas_call(
        functools.partial(_na_attn_kernel, rb=rb, span_rows=span_rows, rows=rows),
        out_shape=jax.ShapeDtypeStruct((n, heads * NA_HEAD_DIM), BF16),
        grid=(heads, nblk),
        in_specs=[
            pl.BlockSpec((tq, NA_HEAD_DIM), lambda h, i: (i, h)),
            pl.BlockSpec((n, NA_HEAD_DIM), lambda h, i: (0, heads + h)),
            pl.BlockSpec((n, NA_HEAD_DIM), lambda h, i: (0, 2 * heads + h)),
            pl.BlockSpec((n_ctx, NA_HEAD_DIM), lambda h, i: (0, heads + h)),
            pl.BlockSpec((n_ctx, NA_HEAD_DIM), lambda h, i: (0, 2 * heads + h)),
            pl.BlockSpec((1, 2 * NA_KH, LANES), lambda h, i: (h, 0, 0)),
        ],
        out_specs=pl.BlockSpec((tq, NA_HEAD_DIM), lambda h, i: (i, h)),
        scratch_shapes=[pltpu.VMEM((3, tq, span_rows * GRID_W), F32)],
        compiler_params=_cparams(("arbitrary", "arbitrary")),
        name="na_attention",
    )(qkv, qkv, qkv, qkv_ctx, qkv_ctx, rpb_pad)


def _outproj_kernel(o_ref, w_ref, x_ref, ga_ref, g_ref, sc_ref, sh_ref, rwh_ref, rwl_ref,
                    xo_ref, f_ref, aff_ref, *, n_experts):
    acc = jnp.dot(o_ref[...], w_ref[...], preferred_element_type=F32)
    xn = x_ref[...] + ga_ref[...] * acc
    xo_ref[...] = xn
    y = xn * lax.rsqrt(jnp.mean(xn * xn, axis=-1, keepdims=True) + EPS) * g_ref[...]
    f = y * (1.0 + sc_ref[...]) + sh_ref[...]
    f_ref[...] = f
    f_hi = f.astype(BF16)
    f_lo = (f - f_hi.astype(F32)).astype(BF16)
    logits = (jnp.dot(f_hi, rwh_ref[...], preferred_element_type=F32)
              + jnp.dot(f_lo, rwh_ref[...], preferred_element_type=F32)
              + jnp.dot(f_hi, rwl_ref[...], preferred_element_type=F32))
    lane = lax.broadcasted_iota(jnp.int32, logits.shape, 1)
    logits = jnp.where(lane < n_experts, logits, NEG_BIG)
    e = jnp.exp(logits - jnp.max(logits, axis=-1, keepdims=True))
    aff_ref[...] = e * (1.0 / jnp.sum(e, axis=-1, keepdims=True))


def outproj_ffnprep(o, w_o, x, gate_a, gain, scale, shift, rw_hi, rw_lo, *, tm, n_experts=N_EXPERTS):
    m, d = x.shape
    k = o.shape[1]
    row = lambda i: (i, 0)
    full = lambda i: (0, 0)
    return pl.pallas_call(
        functools.partial(_outproj_kernel, n_experts=n_experts),
        out_shape=(
            jax.ShapeDtypeStruct((m, d), F32),
            jax.ShapeDtypeStruct((m, d), F32),
            jax.ShapeDtypeStruct((m, LANES), F32),
        ),
        grid=(m // tm,),
        in_specs=[
            pl.BlockSpec((tm, k), row),
            pl.BlockSpec((k, d), full),
            pl.BlockSpec((tm, d), row),
            pl.BlockSpec((1, d), full),
            pl.BlockSpec((1, d), full),
            pl.BlockSpec((1, d), full),
            pl.BlockSpec((1, d), full),
            pl.BlockSpec((d, LANES), full),
            pl.BlockSpec((d, LANES), full),
        ],
        out_specs=(
            pl.BlockSpec((tm, d), row),
            pl.BlockSpec((tm, d), row),
            pl.BlockSpec((tm, LANES), row),
        ),
        compiler_params=_cparams(("arbitrary",)),
        name="outproj_ffnprep",
    )(o, w_o, x, gate_a, gain, scale, shift, rw_hi, rw_lo)


def _prefix_rank(sel_bf, upper_incl, lower_strict):
    w = jnp.dot(sel_bf, upper_incl, preferred_element_type=F32)
    row_tot = jnp.broadcast_to(w[:, LANES - 1:LANES], w.shape)
    row_off = jnp.dot(lower_strict, row_tot.astype(BF16), preferred_element_type=F32)
    return w, row_tot, row_off


def _topk_kernel(a_ref, tok_ref, gate_ref, *, cap, jc):
    a = a_ref[0]
    tb = a.shape[0]
    ri = lax.broadcasted_iota(jnp.int32, (tb, LANES), 0)
    ci = lax.broadcasted_iota(jnp.int32, (tb, LANES), 1)
    upper_incl = (ri <= ci).astype(BF16)
    lower_strict = (ci < ri).astype(BF16)

    def count(mask):
        c = jnp.sum(mask.astype(F32), axis=1, keepdims=True)
        return jnp.sum(c, axis=0, keepdims=True)

    thr = jnp.zeros((1, 1), jnp.int32)
    for bit in range(30, -1, -1):
        cand = thr | jnp.int32(1 << bit)
        candf = lax.bitcast_convert_type(cand, F32)
        thr = jnp.where(count(a >= candf) >= cap, cand, thr)
    thrf = lax.bitcast_convert_type(thr, F32)
    gt = a > thrf
    eq = a == thrf
    need = cap - count(gt)
    w_eq, _, off_eq = _prefix_rank(eq.astype(BF16), upper_incl, lower_strict)
    sel = gt | (eq & (w_eq + off_eq <= need))
    sel_bf = sel.astype(BF16)
    w, row_tot, row_off = _prefix_rank(sel_bf, upper_incl, lower_strict)
    tot_t = row_tot.T
    incl_t = jnp.dot(tot_t.astype(BF16), upper_incl, preferred_element_type=F32)
    excl_t = incl_t - tot_t
    excl_row = excl_t[0:1, :]
    incl_row = incl_t[0:1, :]
    w_bf = w.astype(BF16)
    a1 = a.astype(BF16)
    a2 = (a - a1.astype(F32)).astype(BF16)
    a3 = (a - a1.astype(F32) - a2.astype(F32)).astype(BF16)
    lane_f = lax.broadcasted_iota(jnp.int32, (jc, LANES), 1).astype(F32)
    for c in range(cap // jc):
        slot = (lax.broadcasted_iota(jnp.int32, (jc, 1), 0) + c * jc).astype(F32)
        in_row = (excl_row <= slot) & (slot < incl_row)
        in_row_f = in_row.astype(F32)
        in_row_bf = in_row.astype(BF16)
        w_g = jnp.dot(in_row_bf, w_bf, preferred_element_type=F32)
        s_g = jnp.dot(in_row_bf, sel_bf, preferred_element_type=F32)
        off_g = jnp.sum(in_row_f * excl_row, axis=1, keepdims=True)
        hit = (s_g > 0.5) & (w_g + off_g == slot + 1.0)
        hit_f = hit.astype(F32)
        row_id = jnp.sum(in_row_f * lane_f, axis=1, keepdims=True)
        lane_id = jnp.sum(hit_f * lane_f, axis=1, keepdims=True)
        a_g = (jnp.dot(in_row_bf, a1, preferred_element_type=F32)
               + jnp.dot(in_row_bf, a2, preferred_element_type=F32)
               + jnp.dot(in_row_bf, a3, preferred_element_type=F32))
        tok_ref[0, c * jc:(c + 1) * jc, :] = (row_id * float(LANES) + lane_id).astype(jnp.int32)
        gate_ref[0, c * jc:(c + 1) * jc, :] = jnp.sum(hit_f * a_g, axis=1, keepdims=True)


def expert_choice(aff_t, *, cap):
    e, t = aff_t.shape
    t_pad = LANES * LANES
    a = jnp.pad(aff_t, ((0, 0), (0, t_pad - t)), constant_values=-1.0).reshape(e, LANES, LANES)
    jc = min(cap, 256)
    tok, gate = pl.pallas_call(
        functools.partial(_topk_kernel, cap=cap, jc=jc),
        out_shape=(
            jax.ShapeDtypeStruct((e, cap, 1), jnp.int32),
            jax.ShapeDtypeStruct((e, cap, 1), F32),
        ),
        grid=(e,),
        in_specs=[pl.BlockSpec((1, LANES, LANES), lambda i: (i, 0, 0))],
        out_specs=(
            pl.BlockSpec((1, cap, 1), lambda i: (i, 0, 0)),
            pl.BlockSpec((1, cap, 1), lambda i: (i, 0, 0)),
        ),
        compiler_params=_cparams(("arbitrary",)),
        name="expert_choice",
    )(a)
    return tok.reshape(e, cap), gate


def _moe_kernel(idx_hbm, f_hbm, x_hbm, g_ref, gf_ref, wg_ref, wu_ref, wd_ref, o_hbm,
                idx_s, xs, xo, sem_idx, sem_f, sem_x, sem_o, *, tm, chunks, n_steps):
    del x_hbm
    step = pl.program_id(0) * chunks + pl.program_id(1)
    buf = step % 2
    stride = idx_hbm.shape[1]
    cur, nxt, prv = step % 3, (step + 1) % 3, (step + 2) % 3

    def idx_copy(s, slot):
        return pltpu.make_async_copy(idx_hbm.at[s], idx_s.at[pl.ds(pl.multiple_of(slot * stride, stride), stride)], sem_idx)

    def f_row(g, u, slot, b):
        tok = idx_s[slot * stride + g * SUBLANES + u]
        return pltpu.make_async_copy(f_hbm.at[pl.ds(tok, 1)], xs.at[b, g, pl.ds(u, 1)], sem_f.at[b])

    def x_row_in(g, u):
        tok = idx_s[cur * stride + g * SUBLANES + u]
        return pltpu.make_async_copy(o_hbm.at[pl.ds(tok, 1)], xo.at[buf, g, pl.ds(u, 1)], sem_x)

    def x_row_out(g, u, slot, b):
        tok = idx_s[slot * stride + g * SUBLANES + u]
        return pltpu.make_async_copy(xo.at[b, g, pl.ds(u, 1)], o_hbm.at[pl.ds(tok, 1)], sem_o)

    def rows(fn):
        def body(g, c):
            for u in range(SUBLANES):
                fn(g, u)
            return c
        lax.fori_loop(0, tm // SUBLANES, body, 0)

    @pl.when(step == 0)
    def _():
        idx_copy(0, 0).start()
        idx_copy(0, 0).wait()
        rows(lambda g, u: f_row(g, u, 0, 0).start())

    @pl.when(step + 1 < n_steps)
    def _():
        idx_copy(step + 1, nxt).start()
        idx_copy(step + 1, nxt).wait()
        rows(lambda g, u: f_row(g, u, nxt, 1 - buf).start())

    @pl.when(step > 0)
    def _():
        rows(lambda g, u: x_row_out(g, u, prv, 1 - buf).wait())

    rows(lambda g, u: x_row_in(g, u).start())
    rows(lambda g, u: f_row(g, u, cur, buf).wait())
    d = xs.shape[-1]
    xb = xs[buf].reshape(tm, d).astype(BF16)
    hg = jnp.dot(xb, wg_ref[0], preferred_element_type=F32)
    hu = jnp.dot(xb, wu_ref[0], preferred_element_type=F32)
    hid = (hg * (1.0 / (1.0 + jnp.exp(-hg)))) * hu
    y = jnp.dot(hid.astype(BF16), wd_ref[0], preferred_element_type=F32)
    rows(lambda g, u: x_row_in(g, u).wait())
    upd = gf_ref[...] * (y * g_ref[...])
    xo[buf] = xo[buf] + upd.reshape(tm // SUBLANES, SUBLANES, d)
    rows(lambda g, u: x_row_out(g, u, cur, buf).start())

    @pl.when(step == n_steps - 1)
    def _():
        rows(lambda g, u: x_row_out(g, u, cur, buf).wait())


def moe_apply(f, x, tok, gate, gate_f, wg, wu, wd, *, tm):
    t, d = x.shape
    e, cap = tok.shape
    chunks = cap // tm
    ff = wg.shape[2]
    stride = max(tm, LANES)
    tok_rows = jnp.pad(tok.reshape(e * chunks, tm), ((0, 0), (0, stride - tm)))
    return pl.pallas_call(
        functools.partial(_moe_kernel, tm=tm, chunks=chunks, n_steps=e * chunks),
        out_shape=jax.ShapeDtypeStruct((t, d), F32),
        grid=(e, chunks),
        in_specs=[
            pl.BlockSpec(memory_space=pl.ANY),
            pl.BlockSpec(memory_space=pl.ANY),
            pl.BlockSpec(memory_space=pl.ANY),
            pl.BlockSpec((tm, 1), lambda i, c: (i * chunks + c, 0)),
            pl.BlockSpec((1, d), lambda i, c: (0, 0)),
            pl.BlockSpec((1, d, ff), lambda i, c: (i, 0, 0)),
            pl.BlockSpec((1, d, ff), lambda i, c: (i, 0, 0)),
            pl.BlockSpec((1, ff, d), lambda i, c: (i, 0, 0)),
        ],
        out_specs=pl.BlockSpec(memory_space=pl.ANY),
        scratch_shapes=[
            pltpu.SMEM((3 * stride,), jnp.int32),
            pltpu.VMEM((2, tm // SUBLANES, SUBLANES, d), F32),
            pltpu.VMEM((2, tm // SUBLANES, SUBLANES, d), F32),
            pltpu.SemaphoreType.DMA,
            pltpu.SemaphoreType.DMA((2,)),
            pltpu.SemaphoreType.DMA,
            pltpu.SemaphoreType.DMA,
        ],
        input_output_aliases={2: 0},
        compiler_params=_cparams(("arbitrary", "arbitrary")),
        name="moe_apply",
    )(tok_rows, f, x, gate.reshape(e * cap, 1), gate_f, wg, wu, wd)


def _swap_rot_halves(w):
    q = QK_ROPE // 4
    return jnp.concatenate([w[..., q:2 * q], w[..., :q], w[..., 3 * q:], w[..., 2 * q:3 * q]], axis=-1)


def _pad_lanes(w):
    return jnp.pad(w, [(0, 0)] * (w.ndim - 1) + [(0, LANES - w.shape[-1])])


def _rope_tables(n):
    half = QK_ROPE // 2
    inv = ROPE_THETA ** (-jnp.arange(0, half, 2, dtype=F32) / half)
    t = jnp.arange(n)
    ang_r = (t // GRID_W).astype(F32)[:, None] * inv[None, :]
    ang_c = (t % GRID_W).astype(F32)[:, None] * inv[None, :]
    cr, sr, cc, sc = jnp.cos(ang_r), jnp.sin(ang_r), jnp.cos(ang_c), jnp.sin(ang_c)
    return (_pad_lanes(jnp.concatenate([cr, cr, cc, cc], axis=-1)),
            _pad_lanes(jnp.concatenate([-sr, sr, -sc, sc], axis=-1)))


def _gain_rows(g):
    pe = g[QK_NOPE:]
    rows = jnp.stack([g[:QK_NOPE], _pad_lanes(pe), _pad_lanes(_swap_rot_halves(pe))])
    return jnp.pad(rows, ((0, 5), (0, 0)))


def _moe_layer(f, x, aff, gate_f, wg, wu, wd, *, tm):
    t = x.shape[0]
    cap = EC_FACTOR * t // N_EXPERTS
    tok, gate = expert_choice(aff[:, :N_EXPERTS].T, cap=cap)
    return moe_apply(f, x, tok, gate, gate_f, wg, wu, wd, tm=min(tm, cap))


def kernel(x, c, ctx, c_ctx, ada_w, ada_b, norm_mix, norm_ffn, mla_w_in, mla_q_a_gain, mla_kv_a_gain,
           mla_w_qb, mla_w_kvb, mla_q_gain, mla_k_gain, mla_w_o, na_w_qkv, na_q_gain, na_k_gain, na_rpb,
           na_w_o, router_w, moe_w_gate, moe_w_up, moe_w_down):
    n = x.shape[1]
    n_ctx = ctx.shape[1]
    d = D_MODEL
    x_lat = x[0]
    x_ctx = ctx[0]

    mods = ada_modulation(jnp.stack([c[0], c_ctx]), ada_w, ada_b)

    def mod(layer, r, k):
        return mods[layer, r:r + 1, k * d:(k + 1) * d]

    row = lambda v: v.reshape(1, -1)
    rw_hi = [_pad_lanes(router_w[i]).astype(BF16) for i in range(2)]
    rw_lo = [(_pad_lanes(router_w[i]) - rw_hi[i].astype(F32)).astype(BF16) for i in range(2)]
    wg = [moe_w_gate[i].astype(BF16) for i in range(2)]
    wu = [moe_w_up[i].astype(BF16) for i in range(2)]
    wd = [moe_w_down[i].astype(BF16) for i in range(2)]

    w_in = mla_w_in[0]
    kpe_w = w_in[:, Q_LORA + KV_LORA:]
    w_in_ext = jnp.concatenate(
        [w_in[:, :Q_LORA + KV_LORA], _pad_lanes(kpe_w), _pad_lanes(_swap_rot_halves(kpe_w))], axis=1).astype(BF16)
    wq = mla_w_qb[0].reshape(Q_LORA, MLA_HEADS, QK_HEAD)
    wq_ext = jnp.concatenate(
        [wq[..., :QK_NOPE], _pad_lanes(wq[..., QK_NOPE:]), _pad_lanes(_swap_rot_halves(wq[..., QK_NOPE:]))],
        axis=-1).reshape(Q_LORA, -1).astype(BF16)
    wkv = mla_w_kvb[0].astype(BF16)
    gq = _gain_rows(mla_q_gain[0])
    gk = _gain_rows(mla_k_gain[0])
    rope_c, rope_s = _rope_tables(n)
    ones_c = _pad_lanes(jnp.ones((n_ctx, QK_ROPE), F32))
    zeros_s = jnp.zeros((n_ctx, LANES), F32)
    no_gain = jnp.ones((1, w_in_ext.shape[1]), F32)

    a_lat = norm_mod_proj(x_lat, row(norm_mix[0]), mod(0, 0, 1), mod(0, 0, 0), w_in_ext, no_gain,
                          tm=512, tn=w_in_ext.shape[1], out_dtype=F32)
    a_ctx = norm_mod_proj(x_ctx, row(norm_mix[0]), mod(0, 1, 1), mod(0, 1, 0), w_in_ext, no_gain,
                          tm=n_ctx, tn=w_in_ext.shape[1], out_dtype=F32)
    q_l, k_l, v_l = mla_qkv(a_lat, row(mla_q_a_gain[0]), row(mla_kv_a_gain[0]), wq_ext, wkv, gq, gk,
                            rope_c, rope_s, tm=256)
    q_c, k_c, v_c = mla_qkv(a_ctx, row(mla_q_a_gain[0]), row(mla_kv_a_gain[0]), wq_ext, wkv, gq, gk,
                            ones_c, zeros_s, tm=n_ctx)
    o_lat = mla_attention(q_l, k_c, v_c, k_l, v_l, tq=1024)
    o_ctx = mla_attention(q_c, k_c, v_c, tq=n_ctx)

    w_o = mla_w_o[0].astype(BF16)
    x_lat, f_lat, aff_lat = outproj_ffnprep(o_lat, w_o, x_lat, mod(0, 0, 2), row(norm_ffn[0]), mod(0, 0, 4),
                                            mod(0, 0, 3), rw_hi[0], rw_lo[0], tm=512)
    x_ctx, f_ctx, aff_ctx = outproj_ffnprep(o_ctx, w_o, x_ctx, mod(0, 1, 2), row(norm_ffn[0]), mod(0, 1, 4),
                                            mod(0, 1, 3), rw_hi[0], rw_lo[0], tm=n_ctx)
    x_lat = _moe_layer(f_lat, x_lat, aff_lat, mod(0, 0, 5), wg[0], wu[0], wd[0], tm=256)
    x_ctx = _moe_layer(f_ctx, x_ctx, aff_ctx, mod(0, 1, 5), wg[0], wu[0], wd[0], tm=256)

    w_qkv = na_w_qkv[0].astype(BF16)
    qkv_gain = jnp.concatenate([jnp.tile(na_q_gain[0], NA_HEADS) * NA_HEAD_DIM ** -0.5,
                                jnp.tile(na_k_gain[0], NA_HEADS), jnp.ones((d,), F32)]).reshape(1, -1)
    qkv_l = norm_mod_proj(x_lat, row(norm_mix[1]), mod(1, 0, 1), mod(1, 0, 0), w_qkv, qkv_gain,
                          tm=512, tn=1024, out_dtype=BF16, n_norm_blocks=4)
    qkv_c = norm_mod_proj(x_ctx, row(norm_mix[1]), mod(1, 1, 1), mod(1, 1, 0), w_qkv, qkv_gain,
                          tm=n_ctx, tn=1024, out_dtype=BF16, n_norm_blocks=4)
    o_lat = na_attention(qkv_l, qkv_c, na_rpb[0])
    x_lat, f_lat, aff_lat = outproj_ffnprep(o_lat, na_w_o[0].astype(BF16), x_lat, mod(1, 0, 2), row(norm_ffn[1]),
                                            mod(1, 0, 4), mod(1, 0, 3), rw_hi[1], rw_lo[1], tm=512)
    x_lat = _moe_layer(f_lat, x_lat, aff_lat, mod(1, 0, 5), wg[1], wu[1], wd[1], tm=256)
    return x_lat[None]
```

```python
import functools
import math

import jax
import jax.numpy as jnp
from jax import lax
from jax.experimental import pallas as pl
from jax.experimental.pallas import tpu as pltpu

F32 = jnp.float32
BF16 = jnp.bfloat16

D_MODEL = 2048
GRID_W = 64
EPS = 1e-6
ROPE_THETA = 10000.0

MLA_HEADS = 16
Q_LORA = 512
KV_LORA = 512
QK_NOPE = 128
QK_ROPE = 64
QK_HEAD = QK_NOPE + QK_ROPE
V_DIM = 128
MLA_QK_PAD = 256

NA_HEADS = 16
NA_HEAD_DIM = 128
NA_KH = 8
NA_KW = 16

N_EXPERTS = 16
EXPERT_FF = 1408
EC_FACTOR = 2

LANES = 128
SUBLANES = 8
NEG_BIG = -1e30
LOG2_E = math.log2(math.e)

VMEM_LIMIT = 56 * 1024 * 1024


def _cparams(sem):
    return pltpu.CompilerParams(dimension_semantics=sem, vmem_limit_bytes=VMEM_LIMIT)


def _ada_kernel(c_ref, w_ref, b_ref, o_ref, *, tn):
    for r in range(2):
        c = c_ref[r]
        act = c * (1.0 / (1.0 + jnp.exp(-c)))
        for k in range(tn // LANES):
            sl = slice(k * LANES, (k + 1) * LANES)
            o_ref[0, r:r + 1, sl] = jnp.sum(act * w_ref[0, :, sl], axis=0, keepdims=True) + b_ref[0, :, sl]


def ada_modulation(cvec, ada_w, ada_b, *, tn=512):
    depth, d, n = ada_w.shape
    c_rep = jnp.broadcast_to(cvec[:, :, None], (2, d, LANES))
    return pl.pallas_call(
        functools.partial(_ada_kernel, tn=tn),
        out_shape=jax.ShapeDtypeStruct((depth, 2, n), F32),
        grid=(depth, n // tn),
        in_specs=[
            pl.BlockSpec((2, d, LANES), lambda l, j: (0, 0, 0)),
            pl.BlockSpec((1, d, tn), lambda l, j: (l, 0, j)),
            pl.BlockSpec((1, 1, tn), lambda l, j: (l, 0, j)),
        ],
        out_specs=pl.BlockSpec((1, 2, tn), lambda l, j: (l, 0, j)),
        compiler_params=_cparams(("arbitrary", "arbitrary")),
        name="ada_modulation",
    )(c_rep, ada_w, ada_b.reshape(depth, 1, n))


def _proj_kernel(x_ref, g_ref, sc_ref, sh_ref, w_ref, cg_ref, o_ref, h_ref, *, tn, n_norm_blocks, head):
    j = pl.program_id(1)

    @pl.when(j == 0)
    def _():
        x = x_ref[...]
        y = x * lax.rsqrt(jnp.mean(x * x, axis=-1, keepdims=True) + EPS) * g_ref[...]
        h_ref[...] = (y * (1.0 + sc_ref[...]) + sh_ref[...]).astype(BF16)

    acc = jnp.dot(h_ref[...], w_ref[...], preferred_element_type=F32)
    if n_norm_blocks == 0:
        o_ref[...] = acc.astype(o_ref.dtype)
    else:
        @pl.when(j < n_norm_blocks)
        def _():
            for c in range(tn // head):
                sl = slice(c * head, (c + 1) * head)
                y = acc[:, sl]
                r = lax.rsqrt(jnp.mean(y * y, axis=-1, keepdims=True) + EPS)
                o_ref[:, sl] = (y * r * cg_ref[:, sl]).astype(o_ref.dtype)

        @pl.when(j >= n_norm_blocks)
        def _():
            o_ref[...] = acc.astype(o_ref.dtype)


def norm_mod_proj(x, gain, scale, shift, w, col_gain, *, tm, tn, out_dtype, n_norm_blocks=0, head=LANES):
    m, d = x.shape
    n = w.shape[1]
    return pl.pallas_call(
        functools.partial(_proj_kernel, tn=tn, n_norm_blocks=n_norm_blocks, head=head),
        out_shape=jax.ShapeDtypeStruct((m, n), out_dtype),
        grid=(m // tm, n // tn),
        in_specs=[
            pl.BlockSpec((tm, d), lambda i, j: (i, 0)),
            pl.BlockSpec((1, d), lambda i, j: (0, 0)),
            pl.BlockSpec((1, d), lambda i, j: (0, 0)),
            pl.BlockSpec((1, d), lambda i, j: (0, 0)),
            pl.BlockSpec((d, tn), lambda i, j: (0, j)),
            pl.BlockSpec((1, tn), lambda i, j: (0, j)),
        ],
        out_specs=pl.BlockSpec((tm, tn), lambda i, j: (i, j)),
        scratch_shapes=[pltpu.VMEM((tm, d), BF16)],
        compiler_params=_cparams(("arbitrary", "arbitrary")),
        name="norm_mod_proj",
    )(x, gain, scale, shift, w, col_gain)


def _mla_qkv_kernel(a_ref, qag_ref, kvag_ref, wq_ref, wkv_ref, gq_ref, gk_ref, rc_ref, rs_ref,
                    q_ref, k_ref, v_ref, *, heads, q_scale):
    a = a_ref[...]
    qc = a[:, :Q_LORA]
    kvc = a[:, Q_LORA:Q_LORA + KV_LORA]
    kpe1 = a[:, Q_LORA + KV_LORA:Q_LORA + KV_LORA + LANES]
    kpe2 = a[:, Q_LORA + KV_LORA + LANES:]

    def rms(x, g):
        return x * lax.rsqrt(jnp.mean(x * x, axis=-1, keepdims=True) + EPS) * g

    qn = rms(qc, qag_ref[...]).astype(BF16)
    kvn = rms(kvc, kvag_ref[...]).astype(BF16)
    rc = rc_ref[...]
    rs = rs_ref[...]
    gq_nope, gq1, gq2 = gq_ref[0:1, :], gq_ref[1:2, :], gq_ref[2:3, :]
    gk_nope, gk1, gk2 = gk_ref[0:1, :], gk_ref[1:2, :], gk_ref[2:3, :]
    kpe_ss = jnp.sum(kpe1 * kpe1, axis=-1, keepdims=True)
    kpe_rot = kpe1 * gk1 * rc + kpe2 * gk2 * rs
    qw = QK_NOPE + 2 * LANES
    for h in range(heads):
        qh = jnp.dot(qn, wq_ref[:, h * qw:(h + 1) * qw], preferred_element_type=F32)
        qa, q1, q2 = qh[:, :QK_NOPE], qh[:, QK_NOPE:QK_NOPE + LANES], qh[:, QK_NOPE + LANES:]
        ss = jnp.sum(qa * qa, axis=-1, keepdims=True) + jnp.sum(q1 * q1, axis=-1, keepdims=True)
        r = lax.rsqrt(ss * (1.0 / QK_HEAD) + EPS) * q_scale
        q_ref[:, h * MLA_QK_PAD:h * MLA_QK_PAD + QK_NOPE] = (qa * r * gq_nope).astype(BF16)
        q_ref[:, h * MLA_QK_PAD + QK_NOPE:(h + 1) * MLA_QK_PAD] = (
            (q1 * gq1 * rc + q2 * gq2 * rs) * r).astype(BF16)
        kvh = jnp.dot(kvn, wkv_ref[:, h * (QK_NOPE + V_DIM):(h + 1) * (QK_NOPE + V_DIM)],
                      preferred_element_type=F32)
        kn, vv = kvh[:, :QK_NOPE], kvh[:, QK_NOPE:]
        rk = lax.rsqrt((jnp.sum(kn * kn, axis=-1, keepdims=True) + kpe_ss) * (1.0 / QK_HEAD) + EPS)
        k_ref[:, h * MLA_QK_PAD:h * MLA_QK_PAD + QK_NOPE] = (kn * rk * gk_nope).astype(BF16)
        k_ref[:, h * MLA_QK_PAD + QK_NOPE:(h + 1) * MLA_QK_PAD] = (kpe_rot * rk).astype(BF16)
        v_ref[:, h * V_DIM:(h + 1) * V_DIM] = vv.astype(BF16)


def mla_qkv(a, q_a_gain, kv_a_gain, wq_ext, wkv, gq, gk, rope_c, rope_s, *, tm, heads=MLA_HEADS):
    m, aw = a.shape
    full = lambda i: (0, 0)
    return pl.pallas_call(
        functools.partial(_mla_qkv_kernel, heads=heads, q_scale=QK_HEAD ** -0.5 * LOG2_E),
        out_shape=(
            jax.ShapeDtypeStruct((m, heads * MLA_QK_PAD), BF16),
            jax.ShapeDtypeStruct((m, heads * MLA_QK_PAD), BF16),
            jax.ShapeDtypeStruct((m, heads * V_DIM), BF16),
        ),
        grid=(m // tm,),
        in_specs=[
            pl.BlockSpec((tm, aw), lambda i: (i, 0)),
            pl.BlockSpec((1, Q_LORA), full),
            pl.BlockSpec((1, KV_LORA), full),
            pl.BlockSpec(wq_ext.shape, full),
            pl.BlockSpec(wkv.shape, full),
            pl.BlockSpec((8, LANES), full),
            pl.BlockSpec((8, LANES), full),
            pl.BlockSpec((tm, LANES), lambda i: (i, 0)),
            pl.BlockSpec((tm, LANES), lambda i: (i, 0)),
        ],
        out_specs=(
            pl.BlockSpec((tm, heads * MLA_QK_PAD), lambda i: (i, 0)),
            pl.BlockSpec((tm, heads * MLA_QK_PAD), lambda i: (i, 0)),
            pl.BlockSpec((tm, heads * V_DIM), lambda i: (i, 0)),
        ),
        compiler_params=_cparams(("arbitrary",)),
        name="mla_qkv",
    )(a, q_a_gain, kv_a_gain, wq_ext, wkv, gq, gk, rope_c, rope_s)


def _nt_dot(a, b):
    return lax.dot_general(a, b, (((1,), (1,)), ((), ())), preferred_element_type=F32)


def _mla_attn_kernel(*refs, tk, n_lat):
    if n_lat:
        q_ref, kc_ref, vc_ref, kl_ref, vl_ref, o_ref = refs
    else:
        q_ref, kc_ref, vc_ref, o_ref = refs
    tq = q_ref.shape[0]

    def step(k, v, m, l, acc):
        s = _nt_dot(q_ref[...], k)
        m_new = jnp.maximum(m, jnp.max(s, axis=-1, keepdims=True))
        p = jnp.exp2(s - m_new)
        alpha = jnp.exp2(m - m_new)
        l_new = alpha * l + jnp.sum(p, axis=-1, keepdims=True)
        acc_new = alpha * acc + jnp.dot(p.astype(BF16), v, preferred_element_type=F32)
        return m_new, l_new, acc_new

    m0 = jnp.full((tq, 1), NEG_BIG, F32)
    l0 = jnp.zeros((tq, 1), F32)
    acc0 = jnp.zeros((tq, V_DIM), F32)
    carry = step(kc_ref[...], vc_ref[...], m0, l0, acc0)
    if n_lat:
        def body(i, carry):
            start = pl.multiple_of(i * tk, tk)
            return step(kl_ref[pl.ds(start, tk), :], vl_ref[pl.ds(start, tk), :], *carry)

        carry = lax.fori_loop(0, n_lat // tk, body, carry, unroll=2)
    _, l, acc = carry
    o_ref[...] = (acc * (1.0 / l)).astype(o_ref.dtype)


def mla_attention(q, k_ctx, v_ctx, k_lat=None, v_lat=None, *, tq, tk=512, heads=MLA_HEADS):
    nq = q.shape[0]
    n_ctx = k_ctx.shape[0]
    n_lat = 0 if k_lat is None else k_lat.shape[0]
    in_specs = [
        pl.BlockSpec((tq, MLA_QK_PAD), lambda h, i: (i, h)),
        pl.BlockSpec((n_ctx, MLA_QK_PAD), lambda h, i: (0, h)),
        pl.BlockSpec((n_ctx, V_DIM), lambda h, i: (0, h)),
    ]
    args = [q, k_ctx, v_ctx]
    if n_lat:
        in_specs += [
            pl.BlockSpec((n_lat, MLA_QK_PAD), lambda h, i: (0, h)),
            pl.BlockSpec((n_lat, V_DIM), lambda h, i: (0, h)),
        ]
        args += [k_lat, v_lat]
    return pl.pallas_call(
        functools.partial(_mla_attn_kernel, tk=tk, n_lat=n_lat),
        out_shape=jax.ShapeDtypeStruct((nq, heads * V_DIM), BF16),
        grid=(heads, nq // tq),
        in_specs=in_specs,
        out_specs=pl.BlockSpec((tq, V_DIM), lambda h, i: (i, h)),
        compiler_params=_cparams(("arbitrary", "arbitrary")),
        name="mla_attention",
    )(*args)


def _na_build_bias(rpb_ref, bias_sc, *, rb, span_rows, rows):
    nblk = rows // rb
    qc = lax.broadcasted_iota(jnp.int32, (GRID_W, LANES), 0)
    lane = lax.broadcasted_iota(jnp.int32, (GRID_W, LANES), 1)
    kc = lane & (GRID_W - 1)
    col_start = jnp.clip(qc - NA_KW // 2, 0, GRID_W - NA_KW)
    col_ok = (kc >= col_start) & (kc < col_start + NA_KW)
    first = lane < GRID_W
    neg = jnp.full((GRID_W, LANES), NEG_BIG, F32)

    def toeplitz(dr, shift):
        w = jnp.broadcast_to(rpb_ref[0, dr + NA_KH - 1:dr + NA_KH, :] * LOG2_E, (GRID_W, LANES))
        return pltpu.roll(w, shift, 1, stride=1, stride_axis=0)

    pairs = {}

    def pair(dr, ok0, ok1):
        key = (dr, ok0, ok1)
        if key not in pairs:
            a = toeplitz(dr, LANES - (NA_KW - 1)) if ok0 else neg
            b = toeplitz(dr + 1, GRID_W - (NA_KW - 1)) if ok1 else neg
            pairs[key] = jnp.where(col_ok, jnp.where(first, a, b), NEG_BIG)
        return pairs[key]

    for v, blk in enumerate((0, 1, nblk - 1)):
        kr0 = min(max(blk * rb - NA_KH // 2, 0), rows - span_rows)
        for qr in range(rb):
            r = blk * rb + qr
            r0 = min(max(r - NA_KH // 2, 0), rows - NA_KH)
            for m in range(span_rows // 2):
                ka = kr0 + 2 * m
                ok0 = r0 <= ka < r0 + NA_KH
                ok1 = r0 <= ka + 1 < r0 + NA_KH
                blk_val = pair(ka - r, ok0, ok1) if (ok0 or ok1) else neg
                bias_sc[v, qr * GRID_W:(qr + 1) * GRID_W, m * LANES:(m + 1) * LANES] = blk_val


def _na_attn_kernel(q_ref, k_ref, v_ref, kc_ref, vc_ref, rpb_ref, o_ref, bias_sc, *, rb, span_rows, rows):
    i = pl.program_id(1)
    nblk = rows // rb

    @pl.when(i == 0)
    def _():
        _na_build_bias(rpb_ref, bias_sc, rb=rb, span_rows=span_rows, rows=rows)

    kr0 = jnp.clip(i * rb - NA_KH // 2, 0, rows - span_rows)
    start = pl.multiple_of(kr0 * GRID_W, (NA_KH // 2) * GRID_W)
    span = span_rows * GRID_W
    variant = jnp.where(i == 0, 0, jnp.where(i == nblk - 1, 2, 1))
    q = q_ref[...]
    kw = k_ref[pl.ds(start, span), :]
    vw = v_ref[pl.ds(start, span), :]
    s_win = _nt_dot(q, kw) + bias_sc[variant]
    s_ctx = _nt_dot(q, kc_ref[...])
    m = jnp.maximum(jnp.max(s_win, axis=-1, keepdims=True), jnp.max(s_ctx, axis=-1, keepdims=True))
    p_win = jnp.exp2(s_win - m)
    p_ctx = jnp.exp2(s_ctx - m)
    l = jnp.sum(p_win, axis=-1, keepdims=True) + jnp.sum(p_ctx, axis=-1, keepdims=True)
    o = (jnp.dot(p_win.astype(BF16), vw, preferred_element_type=F32)
         + jnp.dot(p_ctx.astype(BF16), vc_ref[...], preferred_element_type=F32))
    o_ref[...] = (o * (1.0 / l)).astype(o_ref.dtype)


def na_attention(qkv, qkv_ctx, rpb, *, rb=8, heads=NA_HEADS):
    n = qkv.shape[0]
    n_ctx = qkv_ctx.shape[0]
    rows = n // GRID_W
    span_rows = rb + NA_KH
    nblk = rows // rb
    tq = rb * GRID_W
    rpb_pad = jnp.pad(rpb, ((0, 0), (0, 2 * NA_KH - rpb.shape[1]), (0, LANES - rpb.shape[2])))
    return pl.pallas_call(
        functools.partial(_na_attn_kernel, rb=rb, span_rows=span_rows, rows=rows),
        out_shape=jax.ShapeDtypeStruct((n, heads * NA_HEAD_DIM), BF16),
        grid=(heads, nblk),
        in_specs=[
            pl.BlockSpec((tq, NA_HEAD_DIM), lambda h, i: (i, h)),
            pl.BlockSpec((n, NA_HEAD_DIM), lambda h, i: (0, heads + h)),
            pl.BlockSpec((n, NA_HEAD_DIM), lambda h, i: (0, 2 * heads + h)),
            pl.BlockSpec((n_ctx, NA_HEAD_DIM), lambda h, i: (0, heads + h)),
            pl.BlockSpec((n_ctx, NA_HEAD_DIM), lambda h, i: (0, 2 * heads + h)),
            pl.BlockSpec((1, 2 * NA_KH, LANES), lambda h, i: (h, 0, 0)),
        ],
        out_specs=pl.BlockSpec((tq, NA_HEAD_DIM), lambda h, i: (i, h)),
        scratch_shapes=[pltpu.VMEM((3, tq, span_rows * GRID_W), F32)],
        compiler_params=_cparams(("arbitrary", "arbitrary")),
        name="na_attention",
    )(qkv, qkv, qkv, qkv_ctx, qkv_ctx, rpb_pad)


def _outproj_kernel(o_ref, w_ref, x_ref, ga_ref, g_ref, sc_ref, sh_ref, rwh_ref, rwl_ref,
                    xo_ref, f_ref, aff_ref, *, n_experts):
    acc = jnp.dot(o_ref[...], w_ref[...], preferred_element_type=F32)
    xn = x_ref[...] + ga_ref[...] * acc
    xo_ref[...] = xn
    y = xn * lax.rsqrt(jnp.mean(xn * xn, axis=-1, keepdims=True) + EPS) * g_ref[...]
    f = y * (1.0 + sc_ref[...]) + sh_ref[...]
    f_ref[...] = f
    f_hi = f.astype(BF16)
    f_lo = (f - f_hi.astype(F32)).astype(BF16)
    logits = (jnp.dot(f_hi, rwh_ref[...], preferred_element_type=F32)
              + jnp.dot(f_lo, rwh_ref[...], preferred_element_type=F32)
              + jnp.dot(f_hi, rwl_ref[...], preferred_element_type=F32))
    lane = lax.broadcasted_iota(jnp.int32, logits.shape, 1)
    logits = jnp.where(lane < n_experts, logits, NEG_BIG)
    e = jnp.exp(logits - jnp.max(logits, axis=-1, keepdims=True))
    aff_ref[...] = e * (1.0 / jnp.sum(e, axis=-1, keepdims=True))


def outproj_ffnprep(o, w_o, x, gate_a, gain, scale, shift, rw_hi, rw_lo, *, tm, n_experts=N_EXPERTS):
    m, d = x.shape
    k = o.shape[1]
    row = lambda i: (i, 0)
    full = lambda i: (0, 0)
    return pl.pallas_call(
        functools.partial(_outproj_kernel, n_experts=n_experts),
        out_shape=(
            jax.ShapeDtypeStruct((m, d), F32),
            jax.ShapeDtypeStruct((m, d), F32),
            jax.ShapeDtypeStruct((m, LANES), F32),
        ),
        grid=(m // tm,),
        in_specs=[
            pl.BlockSpec((tm, k), row),
            pl.BlockSpec((k, d), full),
            pl.BlockSpec((tm, d), row),
            pl.BlockSpec((1, d), full),
            pl.BlockSpec((1, d), full),
            pl.BlockSpec((1, d), full),
            pl.BlockSpec((1, d), full),
            pl.BlockSpec((d, LANES), full),
            pl.BlockSpec((d, LANES), full),
        ],
        out_specs=(
            pl.BlockSpec((tm, d), row),
            pl.BlockSpec((tm, d), row),
            pl.BlockSpec((tm, LANES), row),
        ),
        compiler_params=_cparams(("arbitrary",)),
        name="outproj_ffnprep",
    )(o, w_o, x, gate_a, gain, scale, shift, rw_hi, rw_lo)


def _prefix_rank(sel_bf, upper_incl, lower_strict):
    w = jnp.dot(sel_bf, upper_incl, preferred_element_type=F32)
    row_tot = jnp.broadcast_to(w[:, LANES - 1:LANES], w.shape)
    row_off = jnp.dot(lower_strict, row_tot.astype(BF16), preferred_element_type=F32)
    return w, row_tot, row_off


def _topk_kernel(a_ref, tok_ref, gate_ref, *, cap, jc):
    a = a_ref[0]
    tb = a.shape[0]
    ri = lax.broadcasted_iota(jnp.int32, (tb, LANES), 0)
    ci = lax.broadcasted_iota(jnp.int32, (tb, LANES), 1)
    upper_incl = (ri <= ci).astype(BF16)
    lower_strict = (ci < ri).astype(BF16)

    def count(mask):
        c = jnp.sum(mask.astype(F32), axis=1, keepdims=True)
        return jnp.sum(c, axis=0, keepdims=True)

    thr = jnp.zeros((1, 1), jnp.int32)
    for bit in range(30, -1, -1):
        cand = thr | jnp.int32(1 << bit)
        candf = lax.bitcast_convert_type(cand, F32)
        thr = jnp.where(count(a >= candf) >= cap, cand, thr)
    thrf = lax.bitcast_convert_type(thr, F32)
    gt = a > thrf
    eq = a == thrf
    need = cap - count(gt)
    w_eq, _, off_eq = _prefix_rank(eq.astype(BF16), upper_incl, lower_strict)
    sel = gt | (eq & (w_eq + off_eq <= need))
    sel_bf = sel.astype(BF16)
    w, row_tot, row_off = _prefix_rank(sel_bf, upper_incl, lower_strict)
    tot_t = row_tot.T
    incl_t = jnp.dot(tot_t.astype(BF16), upper_incl, preferred_element_type=F32)
    excl_t = incl_t - tot_t
    excl_row = excl_t[0:1, :]
    incl_row = incl_t[0:1, :]
    w_bf = w.astype(BF16)
    a1 = a.astype(BF16)
    a2 = (a - a1.astype(F32)).astype(BF16)
    a3 = (a - a1.astype(F32) - a2.astype(F32)).astype(BF16)
    lane_f = lax.broadcasted_iota(jnp.int32, (jc, LANES), 1).astype(F32)
    for c in range(cap // jc):
        slot = (lax.broadcasted_iota(jnp.int32, (jc, 1), 0) + c * jc).astype(F32)
        in_row = (excl_row <= slot) & (slot < incl_row)
        in_row_f = in_row.astype(F32)
        in_row_bf = in_row.astype(BF16)
        w_g = jnp.dot(in_row_bf, w_bf, preferred_element_type=F32)
        s_g = jnp.dot(in_row_bf, sel_bf, preferred_element_type=F32)
        off_g = jnp.sum(in_row_f * excl_row, axis=1, keepdims=True)
        hit = (s_g > 0.5) & (w_g + off_g == slot + 1.0)
        hit_f = hit.astype(F32)
        row_id = jnp.sum(in_row_f * lane_f, axis=1, keepdims=True)
        lane_id = jnp.sum(hit_f * lane_f, axis=1, keepdims=True)
        a_g = (jnp.dot(in_row_bf, a1, preferred_element_type=F32)
               + jnp.dot(in_row_bf, a2, preferred_element_type=F32)
               + jnp.dot(in_row_bf, a3, preferred_element_type=F32))
        tok_ref[0, c * jc:(c + 1) * jc, :] = (row_id * float(LANES) + lane_id).astype(jnp.int32)
        gate_ref[0, c * jc:(c + 1) * jc, :] = jnp.sum(hit_f * a_g, axis=1, keepdims=True)


def expert_choice(aff_t, *, cap):
    e, t = aff_t.shape
    t_pad = LANES * LANES
    a = jnp.pad(aff_t, ((0, 0), (0, t_pad - t)), constant_values=-1.0).reshape(e, LANES, LANES)
    jc = min(cap, 256)
    tok, gate = pl.pallas_call(
        functools.partial(_topk_kernel, cap=cap, jc=jc),
        out_shape=(
            jax.ShapeDtypeStruct((e, cap, 1), jnp.int32),
            jax.ShapeDtypeStruct((e, cap, 1), F32),
        ),
        grid=(e,),
        in_specs=[pl.BlockSpec((1, LANES, LANES), lambda i: (i, 0, 0))],
        out_specs=(
            pl.BlockSpec((1, cap, 1), lambda i: (i, 0, 0)),
            pl.BlockSpec((1, cap, 1), lambda i: (i, 0, 0)),
        ),
        compiler_params=_cparams(("arbitrary",)),
        name="expert_choice",
    )(a)
    return tok.reshape(e, cap), gate


def _moe_kernel(idx_hbm, f_hbm, x_hbm, g_ref, gf_ref, wg_ref, wu_ref, wd_ref, o_hbm,
                idx_s, xs, xo, sem_idx, sem_f, sem_x, sem_o, *, tm, chunks, n_steps):
    del x_hbm
    step = pl.program_id(0) * chunks + pl.program_id(1)
    buf = step % 2
    stride = idx_hbm.shape[1]
    cur, nxt, prv = step % 3, (step + 1) % 3, (step + 2) % 3

    def idx_copy(s, slot):
        return pltpu.make_async_copy(idx_hbm.at[s], idx_s.at[pl.ds(pl.multiple_of(slot * stride, stride), stride)], sem_idx)

    def f_row(g, u, slot, b):
        tok = idx_s[slot * stride + g * SUBLANES + u]
        return pltpu.make_async_copy(f_hbm.at[pl.ds(tok, 1)], xs.at[b, g, pl.ds(u, 1)], sem_f.at[b])

    def x_row_in(g, u):
        tok = idx_s[cur * stride + g * SUBLANES + u]
        return pltpu.make_async_copy(o_hbm.at[pl.ds(tok, 1)], xo.at[buf, g, pl.ds(u, 1)], sem_x)

    def x_row_out(g, u, slot, b):
        tok = idx_s[slot * stride + g * SUBLANES + u]
        return pltpu.make_async_copy(xo.at[b, g, pl.ds(u, 1)], o_hbm.at[pl.ds(tok, 1)], sem_o)

    def rows(fn):
        def body(g, c):
            for u in range(SUBLANES):
                fn(g, u)
            return c
        lax.fori_loop(0, tm // SUBLANES, body, 0)

    @pl.when(step == 0)
    def _():
        idx_copy(0, 0).start()
        idx_copy(0, 0).wait()
        rows(lambda g, u: f_row(g, u, 0, 0).start(priority=u % 2))

    @pl.when(step + 1 < n_steps)
    def _():
        idx_copy(step + 1, nxt).start()
        idx_copy(step + 1, nxt).wait()
        rows(lambda g, u: f_row(g, u, nxt, 1 - buf).start(priority=u % 2))

    first_chunk = pl.program_id(1) == 0

    @pl.when((step > 0) & first_chunk)
    def _():
        rows(lambda g, u: x_row_out(g, u, prv, 1 - buf).wait())

    rows(lambda g, u: x_row_in(g, u).start(priority=u % 2))
    rows(lambda g, u: f_row(g, u, cur, buf).wait())
    d = xs.shape[-1]
    xb = xs[buf].reshape(tm, d).astype(BF16)
    hg = jnp.dot(xb, wg_ref[0, 0], preferred_element_type=F32)
    hu = jnp.dot(xb, wu_ref[0, 0], preferred_element_type=F32)
    hid = (hg * (1.0 / (1.0 + jnp.exp(-hg)))) * hu
    y = jnp.dot(hid.astype(BF16), wd_ref[0, 0], preferred_element_type=F32)
    rows(lambda g, u: x_row_in(g, u).wait())
    upd = gf_ref[...] * (y * g_ref[...])
    xo[buf] = xo[buf] + upd.reshape(tm // SUBLANES, SUBLANES, d)

    @pl.when((step > 0) & jnp.logical_not(first_chunk))
    def _():
        rows(lambda g, u: x_row_out(g, u, prv, 1 - buf).wait())

    rows(lambda g, u: x_row_out(g, u, cur, buf).start(priority=u % 2))

    @pl.when(step == n_steps - 1)
    def _():
        rows(lambda g, u: x_row_out(g, u, cur, buf).wait())


def moe_apply(f, x, tok, gate, gate_f, wg, wu, wd, *, layer, tm):
    t, d = x.shape
    e, cap = tok.shape
    chunks = cap // tm
    ff = wg.shape[3]
    stride = max(tm, LANES)
    tok_rows = jnp.pad(tok.reshape(e * chunks, tm), ((0, 0), (0, stride - tm)))
    return pl.pallas_call(
        functools.partial(_moe_kernel, tm=tm, chunks=chunks, n_steps=e * chunks),
        out_shape=jax.ShapeDtypeStruct((t, d), F32),
        grid=(e, chunks),
        in_specs=[
            pl.BlockSpec(memory_space=pl.ANY),
            pl.BlockSpec(memory_space=pl.ANY),
            pl.BlockSpec(memory_space=pl.ANY),
            pl.BlockSpec((tm, 1), lambda i, c: (i * chunks + c, 0)),
            pl.BlockSpec((1, d), lambda i, c: (0, 0)),
            pl.BlockSpec((1, 1, d, ff), lambda i, c: (layer, i, 0, 0)),
            pl.BlockSpec((1, 1, d, ff), lambda i, c: (layer, i, 0, 0)),
            pl.BlockSpec((1, 1, ff, d), lambda i, c: (layer, i, 0, 0)),
        ],
        out_specs=pl.BlockSpec(memory_space=pl.ANY),
        scratch_shapes=[
            pltpu.SMEM((3 * stride,), jnp.int32),
            pltpu.VMEM((2, tm // SUBLANES, SUBLANES, d), F32),
            pltpu.VMEM((2, tm // SUBLANES, SUBLANES, d), F32),
            pltpu.SemaphoreType.DMA,
            pltpu.SemaphoreType.DMA((2,)),
            pltpu.SemaphoreType.DMA,
            pltpu.SemaphoreType.DMA,
        ],
        input_output_aliases={2: 0},
        compiler_params=_cparams(("arbitrary", "arbitrary")),
        name="moe_apply",
    )(tok_rows, f, x, gate.reshape(e * cap, 1), gate_f, wg, wu, wd)


def _swap_rot_halves(w):
    q = QK_ROPE // 4
    return jnp.concatenate([w[..., q:2 * q], w[..., :q], w[..., 3 * q:], w[..., 2 * q:3 * q]], axis=-1)


def _pad_lanes(w):
    return jnp.pad(w, [(0, 0)] * (w.ndim - 1) + [(0, LANES - w.shape[-1])])


def _rope_tables(n):
    half = QK_ROPE // 2
    inv = ROPE_THETA ** (-jnp.arange(0, half, 2, dtype=F32) / half)
    t = jnp.arange(n)
    ang_r = (t // GRID_W).astype(F32)[:, None] * inv[None, :]
    ang_c = (t % GRID_W).astype(F32)[:, None] * inv[None, :]
    cr, sr, cc, sc = jnp.cos(ang_r), jnp.sin(ang_r), jnp.cos(ang_c), jnp.sin(ang_c)
    return (_pad_lanes(jnp.concatenate([cr, cr, cc, cc], axis=-1)),
            _pad_lanes(jnp.concatenate([-sr, sr, -sc, sc], axis=-1)))


def _gain_rows(g):
    pe = g[QK_NOPE:]
    rows = jnp.stack([g[:QK_NOPE], _pad_lanes(pe), _pad_lanes(_swap_rot_halves(pe))])
    return jnp.pad(rows, ((0, 5), (0, 0)))


def _moe_layer(f, x, aff, gate_f, wg, wu, wd, *, layer, tm):
    t = x.shape[0]
    cap = EC_FACTOR * t // N_EXPERTS
    tok, gate = expert_choice(aff[:, :N_EXPERTS].T, cap=cap)
    return moe_apply(f, x, tok, gate, gate_f, wg, wu, wd, layer=layer, tm=min(tm, cap))


def kernel(x, c, ctx, c_ctx, ada_w, ada_b, norm_mix, norm_ffn, mla_w_in, mla_q_a_gain, mla_kv_a_gain,
           mla_w_qb, mla_w_kvb, mla_q_gain, mla_k_gain, mla_w_o, na_w_qkv, na_q_gain, na_k_gain, na_rpb,
           na_w_o, router_w, moe_w_gate, moe_w_up, moe_w_down):
    n = x.shape[1]
    n_ctx = ctx.shape[1]
    d = D_MODEL
    x_lat = x[0]
    x_ctx = ctx[0]

    mods = ada_modulation(jnp.stack([c[0], c_ctx]), ada_w, ada_b)

    def mod(layer, r, k):
        return mods[layer, r:r + 1, k * d:(k + 1) * d]

    row = lambda v: v.reshape(1, -1)
    rw_hi = [_pad_lanes(router_w[i]).astype(BF16) for i in range(2)]
    rw_lo = [(_pad_lanes(router_w[i]) - rw_hi[i].astype(F32)).astype(BF16) for i in range(2)]
    wg = moe_w_gate.astype(BF16)
    wu = moe_w_up.astype(BF16)
    wd = moe_w_down.astype(BF16)

    w_in = mla_w_in[0]
    kpe_w = w_in[:, Q_LORA + KV_LORA:]
    w_in_ext = jnp.concatenate(
        [w_in[:, :Q_LORA + KV_LORA], _pad_lanes(kpe_w), _pad_lanes(_swap_rot_halves(kpe_w))], axis=1).astype(BF16)
    wq = mla_w_qb[0].reshape(Q_LORA, MLA_HEADS, QK_HEAD)
    wq_ext = jnp.concatenate(
        [wq[..., :QK_NOPE], _pad_lanes(wq[..., QK_NOPE:]), _pad_lanes(_swap_rot_halves(wq[..., QK_NOPE:]))],
        axis=-1).reshape(Q_LORA, -1).astype(BF16)
    wkv = mla_w_kvb[0].astype(BF16)
    gq = _gain_rows(mla_q_gain[0])
    gk = _gain_rows(mla_k_gain[0])
    rope_c, rope_s = _rope_tables(n)
    ones_c = _pad_lanes(jnp.ones((n_ctx, QK_ROPE), F32))
    zeros_s = jnp.zeros((n_ctx, LANES), F32)
    no_gain = jnp.ones((1, w_in_ext.shape[1]), F32)

    a_lat = norm_mod_proj(x_lat, row(norm_mix[0]), mod(0, 0, 1), mod(0, 0, 0), w_in_ext, no_gain,
                          tm=512, tn=w_in_ext.shape[1], out_dtype=F32)
    a_ctx = norm_mod_proj(x_ctx, row(norm_mix[0]), mod(0, 1, 1), mod(0, 1, 0), w_in_ext, no_gain,
                          tm=n_ctx, tn=w_in_ext.shape[1], out_dtype=F32)
    q_l, k_l, v_l = mla_qkv(a_lat, row(mla_q_a_gain[0]), row(mla_kv_a_gain[0]), wq_ext, wkv, gq, gk,
                            rope_c, rope_s, tm=256)
    q_c, k_c, v_c = mla_qkv(a_ctx, row(mla_q_a_gain[0]), row(mla_kv_a_gain[0]), wq_ext, wkv, gq, gk,
                            ones_c, zeros_s, tm=n_ctx)
    o_lat = mla_attention(q_l, k_c, v_c, k_l, v_l, tq=1024)
    o_ctx = mla_attention(q_c, k_c, v_c, tq=n_ctx)

    w_o = mla_w_o[0].astype(BF16)
    x_lat, f_lat, aff_lat = outproj_ffnprep(o_lat, w_o, x_lat, mod(0, 0, 2), row(norm_ffn[0]), mod(0, 0, 4),
                                            mod(0, 0, 3), rw_hi[0], rw_lo[0], tm=512)
    x_ctx, f_ctx, aff_ctx = outproj_ffnprep(o_ctx, w_o, x_ctx, mod(0, 1, 2), row(norm_ffn[0]), mod(0, 1, 4),
                                            mod(0, 1, 3), rw_hi[0], rw_lo[0], tm=n_ctx)
    x_lat = _moe_layer(f_lat, x_lat, aff_lat, mod(0, 0, 5), wg, wu, wd, layer=0, tm=256)
    x_ctx = _moe_layer(f_ctx, x_ctx, aff_ctx, mod(0, 1, 5), wg, wu, wd, layer=0, tm=256)

    w_qkv = na_w_qkv[0].astype(BF16)
    qkv_gain = jnp.concatenate([jnp.tile(na_q_gain[0], NA_HEADS) * (NA_HEAD_DIM ** -0.5 * LOG2_E),
                                jnp.tile(na_k_gain[0], NA_HEADS), jnp.ones((d,), F32)]).reshape(1, -1)
    qkv_l = norm_mod_proj(x_lat, row(norm_mix[1]), mod(1, 0, 1), mod(1, 0, 0), w_qkv, qkv_gain,
                          tm=1024, tn=1024, out_dtype=BF16, n_norm_blocks=4)
    qkv_c = norm_mod_proj(x_ctx, row(norm_mix[1]), mod(1, 1, 1), mod(1, 1, 0), w_qkv, qkv_gain,
                          tm=n_ctx, tn=1024, out_dtype=BF16, n_norm_blocks=4)
    o_lat = na_attention(qkv_l, qkv_c, na_rpb[0])
    x_lat, f_lat, aff_lat = outproj_ffnprep(o_lat, na_w_o[0].astype(BF16), x_lat, mod(1, 0, 2), row(norm_ffn[1]),
                                            mod(1, 0, 4), mod(1, 0, 3), rw_hi[1], rw_lo[1], tm=512)
    x_lat = _moe_layer(f_lat, x_lat, aff_lat, mod(1, 0, 5), wg, wu, wd, layer=1, tm=256)
    return x_lat[None]
```

```python
import functools
import math

import jax
import jax.numpy as jnp
from jax import lax
from jax.experimental import pallas as pl
from jax.experimental.pallas import tpu as pltpu

F32 = jnp.float32
BF16 = jnp.bfloat16

D_MODEL = 2048
GRID_W = 64
EPS = 1e-6
ROPE_THETA = 10000.0

MLA_HEADS = 16
Q_LORA = 512
KV_LORA = 512
QK_NOPE = 128
QK_ROPE = 64
QK_HEAD = QK_NOPE + QK_ROPE
V_DIM = 128
MLA_QK_PAD = 256

NA_HEADS = 16
NA_HEAD_DIM = 128
NA_KH = 8
NA_KW = 16

N_EXPERTS = 16
EXPERT_FF = 1408
EC_FACTOR = 2

LANES = 128
SUBLANES = 8
NEG_BIG = -1e30
LOG2_E = math.log2(math.e)

VMEM_LIMIT = 56 * 1024 * 1024


def _cparams(sem):
    return pltpu.CompilerParams(dimension_semantics=sem, vmem_limit_bytes=VMEM_LIMIT)


def _ada_kernel(c_ref, w_ref, b_ref, o_ref, *, tn):
    for r in range(2):
        c = c_ref[r]
        act = c * (1.0 / (1.0 + jnp.exp(-c)))
        for k in range(tn // LANES):
            sl = slice(k * LANES, (k + 1) * LANES)
            o_ref[0, r:r + 1, sl] = jnp.sum(act * w_ref[0, :, sl], axis=0, keepdims=True) + b_ref[0, :, sl]


def ada_modulation(cvec, ada_w, ada_b, *, tn=512):
    depth, d, n = ada_w.shape
    c_rep = jnp.broadcast_to(cvec[:, :, None], (2, d, LANES))
    return pl.pallas_call(
        functools.partial(_ada_kernel, tn=tn),
        out_shape=jax.ShapeDtypeStruct((depth, 2, n), F32),
        grid=(depth, n // tn),
        in_specs=[
            pl.BlockSpec((2, d, LANES), lambda l, j: (0, 0, 0)),
            pl.BlockSpec((1, d, tn), lambda l, j: (l, 0, j)),
            pl.BlockSpec((1, 1, tn), lambda l, j: (l, 0, j)),
        ],
        out_specs=pl.BlockSpec((1, 2, tn), lambda l, j: (l, 0, j)),
        compiler_params=_cparams(("arbitrary", "arbitrary")),
        name="ada_modulation",
    )(c_rep, ada_w, ada_b.reshape(depth, 1, n))


def _proj_kernel(x_ref, g_ref, sc_ref, sh_ref, w_ref, cg_ref, o_ref, h_ref, *, tn, n_norm_blocks, head):
    j = pl.program_id(1)

    @pl.when(j == 0)
    def _():
        x = x_ref[...]
        y = x * lax.rsqrt(jnp.mean(x * x, axis=-1, keepdims=True) + EPS) * g_ref[...]
        h_ref[...] = (y * (1.0 + sc_ref[...]) + sh_ref[...]).astype(BF16)

    acc = jnp.dot(h_ref[...], w_ref[...], preferred_element_type=F32)
    if n_norm_blocks == 0:
        o_ref[...] = acc.astype(o_ref.dtype)
    else:
        @pl.when(j < n_norm_blocks)
        def _():
            for c in range(tn // head):
                sl = slice(c * head, (c + 1) * head)
                y = acc[:, sl]
                r = lax.rsqrt(jnp.mean(y * y, axis=-1, keepdims=True) + EPS)
                o_ref[:, sl] = (y * r * cg_ref[:, sl]).astype(o_ref.dtype)

        @pl.when(j >= n_norm_blocks)
        def _():
            o_ref[...] = acc.astype(o_ref.dtype)


def norm_mod_proj(x, gain, scale, shift, w, col_gain, *, tm, tn, out_dtype, n_norm_blocks=0, head=LANES):
    m, d = x.shape
    n = w.shape[1]
    return pl.pallas_call(
        functools.partial(_proj_kernel, tn=tn, n_norm_blocks=n_norm_blocks, head=head),
        out_shape=jax.ShapeDtypeStruct((m, n), out_dtype),
        grid=(m // tm, n // tn),
        in_specs=[
            pl.BlockSpec((tm, d), lambda i, j: (i, 0)),
            pl.BlockSpec((1, d), lambda i, j: (0, 0)),
            pl.BlockSpec((1, d), lambda i, j: (0, 0)),
            pl.BlockSpec((1, d), lambda i, j: (0, 0)),
            pl.BlockSpec((d, tn), lambda i, j: (0, j)),
            pl.BlockSpec((1, tn), lambda i, j: (0, j)),
        ],
        out_specs=pl.BlockSpec((tm, tn), lambda i, j: (i, j)),
        scratch_shapes=[pltpu.VMEM((tm, d), BF16)],
        compiler_params=_cparams(("arbitrary", "arbitrary")),
        name="norm_mod_proj",
    )(x, gain, scale, shift, w, col_gain)


def _mla_qkv_kernel(a_ref, qag_ref, kvag_ref, wq_ref, wkv_ref, gq_ref, gk_ref, rc_ref, rs_ref,
                    q_ref, k_ref, v_ref, *, heads, q_scale):
    a = a_ref[...]
    qc = a[:, :Q_LORA]
    kvc = a[:, Q_LORA:Q_LORA + KV_LORA]
    kpe1 = a[:, Q_LORA + KV_LORA:Q_LORA + KV_LORA + LANES]
    kpe2 = a[:, Q_LORA + KV_LORA + LANES:]

    def rms(x, g):
        return x * lax.rsqrt(jnp.mean(x * x, axis=-1, keepdims=True) + EPS) * g

    qn = rms(qc, qag_ref[...]).astype(BF16)
    kvn = rms(kvc, kvag_ref[...]).astype(BF16)
    rc = rc_ref[...]
    rs = rs_ref[...]
    gq_nope, gq1, gq2 = gq_ref[0:1, :], gq_ref[1:2, :], gq_ref[2:3, :]
    gk_nope, gk1, gk2 = gk_ref[0:1, :], gk_ref[1:2, :], gk_ref[2:3, :]
    kpe_ss = jnp.sum(kpe1 * kpe1, axis=-1, keepdims=True)
    kpe_rot = kpe1 * gk1 * rc + kpe2 * gk2 * rs
    qw = QK_NOPE + 2 * LANES
    for h in range(heads):
        qh = jnp.dot(qn, wq_ref[:, h * qw:(h + 1) * qw], preferred_element_type=F32)
        qa, q1, q2 = qh[:, :QK_NOPE], qh[:, QK_NOPE:QK_NOPE + LANES], qh[:, QK_NOPE + LANES:]
        ss = jnp.sum(qa * qa, axis=-1, keepdims=True) + jnp.sum(q1 * q1, axis=-1, keepdims=True)
        r = lax.rsqrt(ss * (1.0 / QK_HEAD) + EPS) * q_scale
        q_ref[:, h * MLA_QK_PAD:h * MLA_QK_PAD + QK_NOPE] = (qa * r * gq_nope).astype(BF16)
        q_ref[:, h * MLA_QK_PAD + QK_NOPE:(h + 1) * MLA_QK_PAD] = (
            (q1 * gq1 * rc + q2 * gq2 * rs) * r).astype(BF16)
        kvh = jnp.dot(kvn, wkv_ref[:, h * (QK_NOPE + V_DIM):(h + 1) * (QK_NOPE + V_DIM)],
                      preferred_element_type=F32)
        kn, vv = kvh[:, :QK_NOPE], kvh[:, QK_NOPE:]
        rk = lax.rsqrt((jnp.sum(kn * kn, axis=-1, keepdims=True) + kpe_ss) * (1.0 / QK_HEAD) + EPS)
        k_ref[:, h * MLA_QK_PAD:h * MLA_QK_PAD + QK_NOPE] = (kn * rk * gk_nope).astype(BF16)
        k_ref[:, h * MLA_QK_PAD + QK_NOPE:(h + 1) * MLA_QK_PAD] = (kpe_rot * rk).astype(BF16)
        v_ref[:, h * V_DIM:(h + 1) * V_DIM] = vv.astype(BF16)


def mla_qkv(a, q_a_gain, kv_a_gain, wq_ext, wkv, gq, gk, rope_c, rope_s, *, tm, heads=MLA_HEADS):
    m, aw = a.shape
    full = lambda i: (0, 0)
    return pl.pallas_call(
        functools.partial(_mla_qkv_kernel, heads=heads, q_scale=QK_HEAD ** -0.5 * LOG2_E),
        out_shape=(
            jax.ShapeDtypeStruct((m, heads * MLA_QK_PAD), BF16),
            jax.ShapeDtypeStruct((m, heads * MLA_QK_PAD), BF16),
            jax.ShapeDtypeStruct((m, heads * V_DIM), BF16),
        ),
        grid=(m // tm,),
        in_specs=[
            pl.BlockSpec((tm, aw), lambda i: (i, 0)),
            pl.BlockSpec((1, Q_LORA), full),
            pl.BlockSpec((1, KV_LORA), full),
            pl.BlockSpec(wq_ext.shape, full),
            pl.BlockSpec(wkv.shape, full),
            pl.BlockSpec((8, LANES), full),
            pl.BlockSpec((8, LANES), full),
            pl.BlockSpec((tm, LANES), lambda i: (i, 0)),
            pl.BlockSpec((tm, LANES), lambda i: (i, 0)),
        ],
        out_specs=(
            pl.BlockSpec((tm, heads * MLA_QK_PAD), lambda i: (i, 0)),
            pl.BlockSpec((tm, heads * MLA_QK_PAD), lambda i: (i, 0)),
            pl.BlockSpec((tm, heads * V_DIM), lambda i: (i, 0)),
        ),
        compiler_params=_cparams(("arbitrary",)),
        name="mla_qkv",
    )(a, q_a_gain, kv_a_gain, wq_ext, wkv, gq, gk, rope_c, rope_s)


def _nt_dot(a, b):
    return lax.dot_general(a, b, (((1,), (1,)), ((), ())), preferred_element_type=F32)


ATTN_UNROLL = 4


def _mla_attn_kernel(*refs, tk, n_lat):
    if n_lat:
        q_ref, kc_ref, vc_ref, kl_ref, vl_ref, o_ref = refs
    else:
        q_ref, kc_ref, vc_ref, o_ref = refs
    tq = q_ref.shape[0]

    def step(k, v, m, l, acc):
        s = _nt_dot(q_ref[...], k)
        m_new = jnp.maximum(m, jnp.max(s, axis=-1, keepdims=True))
        p = jnp.exp2(s - m_new)
        alpha = jnp.exp2(m - m_new)
        l_new = alpha * l + jnp.sum(p, axis=-1, keepdims=True)
        acc_new = alpha * acc + jnp.dot(p.astype(BF16), v, preferred_element_type=F32)
        return m_new, l_new, acc_new

    m0 = jnp.full((tq, 1), NEG_BIG, F32)
    l0 = jnp.zeros((tq, 1), F32)
    acc0 = jnp.zeros((tq, V_DIM), F32)
    carry = step(kc_ref[...], vc_ref[...], m0, l0, acc0)
    if n_lat:
        def body(i, carry):
            start = pl.multiple_of(i * tk, tk)
            return step(kl_ref[pl.ds(start, tk), :], vl_ref[pl.ds(start, tk), :], *carry)

        carry = lax.fori_loop(0, n_lat // tk, body, carry, unroll=ATTN_UNROLL)
    _, l, acc = carry
    o_ref[...] = (acc * (1.0 / l)).astype(o_ref.dtype)


def mla_attention(q, k_ctx, v_ctx, k_lat=None, v_lat=None, *, tq, tk=512, heads=MLA_HEADS):
    nq = q.shape[0]
    n_ctx = k_ctx.shape[0]
    n_lat = 0 if k_lat is None else k_lat.shape[0]
    in_specs = [
        pl.BlockSpec((tq, MLA_QK_PAD), lambda h, i: (i, h)),
        pl.BlockSpec((n_ctx, MLA_QK_PAD), lambda h, i: (0, h)),
        pl.BlockSpec((n_ctx, V_DIM), lambda h, i: (0, h)),
    ]
    args = [q, k_ctx, v_ctx]
    if n_lat:
        in_specs += [
            pl.BlockSpec((n_lat, MLA_QK_PAD), lambda h, i: (0, h)),
            pl.BlockSpec((n_lat, V_DIM), lambda h, i: (0, h)),
        ]
        args += [k_lat, v_lat]
    return pl.pallas_call(
        functools.partial(_mla_attn_kernel, tk=tk, n_lat=n_lat),
        out_shape=jax.ShapeDtypeStruct((nq, heads * V_DIM), BF16),
        grid=(heads, nq // tq),
        in_specs=in_specs,
        out_specs=pl.BlockSpec((tq, V_DIM), lambda h, i: (i, h)),
        compiler_params=_cparams(("arbitrary", "arbitrary")),
        name="mla_attention",
    )(*args)


def _na_build_bias(rpb_ref, bias_sc, *, rb, span_rows, rows):
    nblk = rows // rb
    qc = lax.broadcasted_iota(jnp.int32, (GRID_W, LANES), 0)
    lane = lax.broadcasted_iota(jnp.int32, (GRID_W, LANES), 1)
    kc = lane & (GRID_W - 1)
    col_start = jnp.clip(qc - NA_KW // 2, 0, GRID_W - NA_KW)
    col_ok = (kc >= col_start) & (kc < col_start + NA_KW)
    first = lane < GRID_W
    neg = jnp.full((GRID_W, LANES), NEG_BIG, F32)

    def toeplitz(dr, shift):
        w = jnp.broadcast_to(rpb_ref[0, dr + NA_KH - 1:dr + NA_KH, :] * LOG2_E, (GRID_W, LANES))
        return pltpu.roll(w, shift, 1, stride=1, stride_axis=0)

    pairs = {}

    def pair(dr, ok0, ok1):
        key = (dr, ok0, ok1)
        if key not in pairs:
            a = toeplitz(dr, LANES - (NA_KW - 1)) if ok0 else neg
            b = toeplitz(dr + 1, GRID_W - (NA_KW - 1)) if ok1 else neg
            pairs[key] = jnp.where(col_ok, jnp.where(first, a, b), NEG_BIG)
        return pairs[key]

    for v, blk in enumerate((0, 1, nblk - 1)):
        kr0 = min(max(blk * rb - NA_KH // 2, 0), rows - span_rows)
        for qr in range(rb):
            r = blk * rb + qr
            r0 = min(max(r - NA_KH // 2, 0), rows - NA_KH)
            for m in range(span_rows // 2):
                ka = kr0 + 2 * m
                ok0 = r0 <= ka < r0 + NA_KH
                ok1 = r0 <= ka + 1 < r0 + NA_KH
                blk_val = pair(ka - r, ok0, ok1) if (ok0 or ok1) else neg
                bias_sc[v, qr * GRID_W:(qr + 1) * GRID_W, m * LANES:(m + 1) * LANES] = blk_val


def _na_attn_kernel(q_ref, k_ref, v_ref, kc_ref, vc_ref, rpb_ref, o_ref, bias_sc, *, rb, span_rows, rows, sub):
    nblk = rows // rb
    tq = rb * GRID_W

    @pl.when(pl.program_id(1) == 0)
    def _():
        _na_build_bias(rpb_ref, bias_sc, rb=rb, span_rows=span_rows, rows=rows)

    span = span_rows * GRID_W
    for u in range(sub):
        i = pl.program_id(1) * sub + u
        kr0 = jnp.clip(i * rb - NA_KH // 2, 0, rows - span_rows)
        start = pl.multiple_of(kr0 * GRID_W, (NA_KH // 2) * GRID_W)
        variant = jnp.where(i == 0, 0, jnp.where(i == nblk - 1, 2, 1))
        q = q_ref[u * tq:(u + 1) * tq, :]
        kw = k_ref[pl.ds(start, span), :]
        vw = v_ref[pl.ds(start, span), :]
        s_win = _nt_dot(q, kw) + bias_sc[variant]
        s_ctx = _nt_dot(q, kc_ref[...])
        m = jnp.maximum(jnp.max(s_win, axis=-1, keepdims=True), jnp.max(s_ctx, axis=-1, keepdims=True))
        p_win = jnp.exp2(s_win - m)
        p_ctx = jnp.exp2(s_ctx - m)
        l = jnp.sum(p_win, axis=-1, keepdims=True) + jnp.sum(p_ctx, axis=-1, keepdims=True)
        o = (jnp.dot(p_win.astype(BF16), vw, preferred_element_type=F32)
             + jnp.dot(p_ctx.astype(BF16), vc_ref[...], preferred_element_type=F32))
        o_ref[u * tq:(u + 1) * tq, :] = (o * (1.0 / l)).astype(o_ref.dtype)


def na_attention(qkv, qkv_ctx, rpb, *, rb=8, sub=4, heads=NA_HEADS):
    n = qkv.shape[0]
    n_ctx = qkv_ctx.shape[0]
    rows = n // GRID_W
    span_rows = rb + NA_KH
    nblk = rows // rb
    tq = rb * GRID_W
    rpb_pad = jnp.pad(rpb, ((0, 0), (0, 2 * NA_KH - rpb.shape[1]), (0, LANES - rpb.shape[2])))
    return pl.pallas_call(
        functools.partial(_na_attn_kernel, rb=rb, span_rows=span_rows, rows=rows, sub=sub),
        out_shape=jax.ShapeDtypeStruct((n, heads * NA_HEAD_DIM), BF16),
        grid=(heads, nblk // sub),
        in_specs=[
            pl.BlockSpec((sub * tq, NA_HEAD_DIM), lambda h, i: (i, h)),
            pl.BlockSpec((n, NA_HEAD_DIM), lambda h, i: (0, heads + h)),
            pl.BlockSpec((n, NA_HEAD_DIM), lambda h, i: (0, 2 * heads + h)),
            pl.BlockSpec((n_ctx, NA_HEAD_DIM), lambda h, i: (0, heads + h)),
            pl.BlockSpec((n_ctx, NA_HEAD_DIM), lambda h, i: (0, 2 * heads + h)),
            pl.BlockSpec((1, 2 * NA_KH, LANES), lambda h, i: (h, 0, 0)),
        ],
        out_specs=pl.BlockSpec((sub * tq, NA_HEAD_DIM), lambda h, i: (i, h)),
        scratch_shapes=[pltpu.VMEM((3, tq, span_rows * GRID_W), F32)],
        compiler_params=_cparams(("arbitrary", "arbitrary")),
        name="na_attention",
    )(qkv, qkv, qkv, qkv_ctx, qkv_ctx, rpb_pad)


def _outproj_kernel(o_ref, w_ref, x_ref, ga_ref, g_ref, sc_ref, sh_ref, rwh_ref, rwl_ref,
                    xo_ref, f_ref, aff_ref, *, n_experts):
    acc = jnp.dot(o_ref[...], w_ref[...], preferred_element_type=F32)
    xn = x_ref[...] + ga_ref[...] * acc
    xo_ref[...] = xn
    y = xn * lax.rsqrt(jnp.mean(xn * xn, axis=-1, keepdims=True) + EPS) * g_ref[...]
    f = y * (1.0 + sc_ref[...]) + sh_ref[...]
    f_ref[...] = f
    f_hi = f.astype(BF16)
    f_lo = (f - f_hi.astype(F32)).astype(BF16)
    logits = (jnp.dot(f_hi, rwh_ref[...], preferred_element_type=F32)
              + jnp.dot(f_lo, rwh_ref[...], preferred_element_type=F32)
              + jnp.dot(f_hi, rwl_ref[...], preferred_element_type=F32))
    lane = lax.broadcasted_iota(jnp.int32, logits.shape, 1)
    logits = jnp.where(lane < n_experts, logits, NEG_BIG)
    e = jnp.exp(logits - jnp.max(logits, axis=-1, keepdims=True))
    aff_ref[...] = e * (1.0 / jnp.sum(e, axis=-1, keepdims=True))


def outproj_ffnprep(o, w_o, x, gate_a, gain, scale, shift, rw_hi, rw_lo, *, tm, n_experts=N_EXPERTS):
    m, d = x.shape
    k = o.shape[1]
    row = lambda i: (i, 0)
    full = lambda i: (0, 0)
    return pl.pallas_call(
        functools.partial(_outproj_kernel, n_experts=n_experts),
        out_shape=(
            jax.ShapeDtypeStruct((m, d), F32),
            jax.ShapeDtypeStruct((m, d), F32),
            jax.ShapeDtypeStruct((m, LANES), F32),
        ),
        grid=(m // tm,),
        in_specs=[
            pl.BlockSpec((tm, k), row),
            pl.BlockSpec((k, d), full),
            pl.BlockSpec((tm, d), row),
            pl.BlockSpec((1, d), full),
            pl.BlockSpec((1, d), full),
            pl.BlockSpec((1, d), full),
            pl.BlockSpec((1, d), full),
            pl.BlockSpec((d, LANES), full),
            pl.BlockSpec((d, LANES), full),
        ],
        out_specs=(
            pl.BlockSpec((tm, d), row),
            pl.BlockSpec((tm, d), row),
            pl.BlockSpec((tm, LANES), row),
        ),
        compiler_params=_cparams(("arbitrary",)),
        name="outproj_ffnprep",
    )(o, w_o, x, gate_a, gain, scale, shift, rw_hi, rw_lo)


def _prefix_rank(sel_bf, upper_incl, lower_strict):
    w = jnp.dot(sel_bf, upper_incl, preferred_element_type=F32)
    row_tot = jnp.broadcast_to(w[:, LANES - 1:LANES], w.shape)
    row_off = jnp.dot(lower_strict, row_tot.astype(BF16), preferred_element_type=F32)
    return w, row_tot, row_off


def _topk_kernel(a_ref, tok_ref, gate_ref, *, cap, jc):
    a = a_ref[0]
    tb = a.shape[0]
    ri = lax.broadcasted_iota(jnp.int32, (tb, LANES), 0)
    ci = lax.broadcasted_iota(jnp.int32, (tb, LANES), 1)
    upper_incl = (ri <= ci).astype(BF16)
    lower_strict = (ci < ri).astype(BF16)

    def count(mask):
        c = jnp.sum(mask.astype(F32), axis=1, keepdims=True)
        return jnp.sum(c, axis=0, keepdims=True)

    thr = jnp.zeros((1, 1), jnp.int32)
    for bit in range(30, -1, -1):
        cand = thr | jnp.int32(1 << bit)
        candf = lax.bitcast_convert_type(cand, F32)
        thr = jnp.where(count(a >= candf) >= cap, cand, thr)
    thrf = lax.bitcast_convert_type(thr, F32)
    gt = a > thrf
    eq = a == thrf
    need = cap - count(gt)
    w_eq, _, off_eq = _prefix_rank(eq.astype(BF16), upper_incl, lower_strict)
    sel = gt | (eq & (w_eq + off_eq <= need))
    sel_bf = sel.astype(BF16)
    w, row_tot, row_off = _prefix_rank(sel_bf, upper_incl, lower_strict)
    tot_t = row_tot.T
    incl_t = jnp.dot(tot_t.astype(BF16), upper_incl, preferred_element_type=F32)
    excl_t = incl_t - tot_t
    excl_row = excl_t[0:1, :]
    incl_row = incl_t[0:1, :]
    w_bf = w.astype(BF16)
    a1 = a.astype(BF16)
    a2 = (a - a1.astype(F32)).astype(BF16)
    a3 = (a - a1.astype(F32) - a2.astype(F32)).astype(BF16)
    lane_f = lax.broadcasted_iota(jnp.int32, (jc, LANES), 1).astype(F32)
    for c in range(cap // jc):
        slot = (lax.broadcasted_iota(jnp.int32, (jc, 1), 0) + c * jc).astype(F32)
        in_row = (excl_row <= slot) & (slot < incl_row)
        in_row_f = in_row.astype(F32)
        in_row_bf = in_row.astype(BF16)
        w_g = jnp.dot(in_row_bf, w_bf, preferred_element_type=F32)
        s_g = jnp.dot(in_row_bf, sel_bf, preferred_element_type=F32)
        off_g = jnp.sum(in_row_f * excl_row, axis=1, keepdims=True)
        hit = (s_g > 0.5) & (w_g + off_g == slot + 1.0)
        hit_f = hit.astype(F32)
        row_id = jnp.sum(in_row_f * lane_f, axis=1, keepdims=True)
        lane_id = jnp.sum(hit_f * lane_f, axis=1, keepdims=True)
        a_g = (jnp.dot(in_row_bf, a1, preferred_element_type=F32)
               + jnp.dot(in_row_bf, a2, preferred_element_type=F32)
               + jnp.dot(in_row_bf, a3, preferred_element_type=F32))
        tok_ref[0, c * jc:(c + 1) * jc, :] = (row_id * float(LANES) + lane_id).astype(jnp.int32)
        gate_ref[0, c * jc:(c + 1) * jc, :] = jnp.sum(hit_f * a_g, axis=1, keepdims=True)


def expert_choice(aff_t, *, cap):
    e, t = aff_t.shape
    t_pad = LANES * LANES
    a = jnp.pad(aff_t, ((0, 0), (0, t_pad - t)), constant_values=-1.0).reshape(e, LANES, LANES)
    jc = min(cap, 256)
    tok, gate = pl.pallas_call(
        functools.partial(_topk_kernel, cap=cap, jc=jc),
        out_shape=(
            jax.ShapeDtypeStruct((e, cap, 1), jnp.int32),
            jax.ShapeDtypeStruct((e, cap, 1), F32),
        ),
        grid=(e,),
        in_specs=[pl.BlockSpec((1, LANES, LANES), lambda i: (i, 0, 0))],
        out_specs=(
            pl.BlockSpec((1, cap, 1), lambda i: (i, 0, 0)),
            pl.BlockSpec((1, cap, 1), lambda i: (i, 0, 0)),
        ),
        compiler_params=_cparams(("arbitrary",)),
        name="expert_choice",
    )(a)
    return tok.reshape(e, cap), gate


def _moe_kernel(idx_hbm, f_hbm, x_hbm, g_ref, gf_ref, wg_ref, wu_ref, wd_ref, o_hbm,
                idx_s, xs, xo, sem_idx, sem_f, sem_x, sem_o, *, tm, chunks, n_steps):
    del x_hbm
    step = pl.program_id(0) * chunks + pl.program_id(1)
    buf = step % 2
    stride = idx_hbm.shape[1]
    cur, nxt, prv = step % 3, (step + 1) % 3, (step + 2) % 3

    def idx_copy(s, slot):
        return pltpu.make_async_copy(idx_hbm.at[s], idx_s.at[pl.ds(pl.multiple_of(slot * stride, stride), stride)], sem_idx)

    def f_row(g, u, slot, b):
        tok = idx_s[slot * stride + g * SUBLANES + u]
        return pltpu.make_async_copy(f_hbm.at[pl.ds(tok, 1)], xs.at[b, g, pl.ds(u, 1)], sem_f.at[b])

    def x_row_in(g, u):
        tok = idx_s[cur * stride + g * SUBLANES + u]
        return pltpu.make_async_copy(o_hbm.at[pl.ds(tok, 1)], xo.at[buf, g, pl.ds(u, 1)], sem_x)

    def x_row_out(g, u, slot, b):
        tok = idx_s[slot * stride + g * SUBLANES + u]
        return pltpu.make_async_copy(xo.at[b, g, pl.ds(u, 1)], o_hbm.at[pl.ds(tok, 1)], sem_o)

    def rows(fn):
        def body(g, c):
            for u in range(SUBLANES):
                fn(g, u)
            return c
        lax.fori_loop(0, tm // SUBLANES, body, 0)

    @pl.when(step == 0)
    def _():
        idx_copy(0, 0).start()
        idx_copy(0, 0).wait()
        rows(lambda g, u: f_row(g, u, 0, 0).start(priority=u % 2))

    @pl.when(step + 1 < n_steps)
    def _():
        idx_copy(step + 1, nxt).start()
        idx_copy(step + 1, nxt).wait()
        rows(lambda g, u: f_row(g, u, nxt, 1 - buf).start(priority=u % 2))

    first_chunk = pl.program_id(1) == 0

    @pl.when((step > 0) & first_chunk)
    def _():
        rows(lambda g, u: x_row_out(g, u, prv, 1 - buf).wait())

    rows(lambda g, u: x_row_in(g, u).start(priority=u % 2))
    rows(lambda g, u: f_row(g, u, cur, buf).wait())
    d = xs.shape[-1]
    xb = xs[buf].reshape(tm, d).astype(BF16)
    hg = jnp.dot(xb, wg_ref[0, 0], preferred_element_type=F32)
    hu = jnp.dot(xb, wu_ref[0, 0], preferred_element_type=F32)
    hid = (hg * (1.0 / (1.0 + jnp.exp(-hg)))) * hu
    y = jnp.dot(hid.astype(BF16), wd_ref[0, 0], preferred_element_type=F32)
    rows(lambda g, u: x_row_in(g, u).wait())
    upd = gf_ref[...] * (y * g_ref[...])
    xo[buf] = xo[buf] + upd.reshape(tm // SUBLANES, SUBLANES, d)

    @pl.when((step > 0) & jnp.logical_not(first_chunk))
    def _():
        rows(lambda g, u: x_row_out(g, u, prv, 1 - buf).wait())

    rows(lambda g, u: x_row_out(g, u, cur, buf).start(priority=u % 2))

    @pl.when(step == n_steps - 1)
    def _():
        rows(lambda g, u: x_row_out(g, u, cur, buf).wait())


def moe_apply(f, x, tok, gate, gate_f, wg, wu, wd, *, layer, tm):
    t, d = x.shape
    e, cap = tok.shape
    chunks = cap // tm
    ff = wg.shape[3]
    stride = max(tm, LANES)
    tok_rows = jnp.pad(tok.reshape(e * chunks, tm), ((0, 0), (0, stride - tm)))
    return pl.pallas_call(
        functools.partial(_moe_kernel, tm=tm, chunks=chunks, n_steps=e * chunks),
        out_shape=jax.ShapeDtypeStruct((t, d), F32),
        grid=(e, chunks),
        in_specs=[
            pl.BlockSpec(memory_space=pl.ANY),
            pl.BlockSpec(memory_space=pl.ANY),
            pl.BlockSpec(memory_space=pl.ANY),
            pl.BlockSpec((tm, 1), lambda i, c: (i * chunks + c, 0)),
            pl.BlockSpec((1, d), lambda i, c: (0, 0)),
            pl.BlockSpec((1, 1, d, ff), lambda i, c: (layer, i, 0, 0)),
            pl.BlockSpec((1, 1, d, ff), lambda i, c: (layer, i, 0, 0)),
            pl.BlockSpec((1, 1, ff, d), lambda i, c: (layer, i, 0, 0)),
        ],
        out_specs=pl.BlockSpec(memory_space=pl.ANY),
        scratch_shapes=[
            pltpu.SMEM((3 * stride,), jnp.int32),
            pltpu.VMEM((2, tm // SUBLANES, SUBLANES, d), F32),
            pltpu.VMEM((2, tm // SUBLANES, SUBLANES, d), F32),
            pltpu.SemaphoreType.DMA,
            pltpu.SemaphoreType.DMA((2,)),
            pltpu.SemaphoreType.DMA,
            pltpu.SemaphoreType.DMA,
        ],
        input_output_aliases={2: 0},
        compiler_params=_cparams(("arbitrary", "arbitrary")),
        name="moe_apply",
    )(tok_rows, f, x, gate.reshape(e * cap, 1), gate_f, wg, wu, wd)


def _swap_rot_halves(w):
    q = QK_ROPE // 4
    return jnp.concatenate([w[..., q:2 * q], w[..., :q], w[..., 3 * q:], w[..., 2 * q:3 * q]], axis=-1)


def _pad_lanes(w):
    return jnp.pad(w, [(0, 0)] * (w.ndim - 1) + [(0, LANES - w.shape[-1])])


def _rope_tables(n):
    half = QK_ROPE // 2
    inv = ROPE_THETA ** (-jnp.arange(0, half, 2, dtype=F32) / half)
    t = jnp.arange(n)
    ang_r = (t // GRID_W).astype(F32)[:, None] * inv[None, :]
    ang_c = (t % GRID_W).astype(F32)[:, None] * inv[None, :]
    cr, sr, cc, sc = jnp.cos(ang_r), jnp.sin(ang_r), jnp.cos(ang_c), jnp.sin(ang_c)
    return (_pad_lanes(jnp.concatenate([cr, cr, cc, cc], axis=-1)),
            _pad_lanes(jnp.concatenate([-sr, sr, -sc, sc], axis=-1)))


def _gain_rows(g):
    pe = g[QK_NOPE:]
    rows = jnp.stack([g[:QK_NOPE], _pad_lanes(pe), _pad_lanes(_swap_rot_halves(pe))])
    return jnp.pad(rows, ((0, 5), (0, 0)))


def _moe_layer(f, x, aff, gate_f, wg, wu, wd, *, layer, tm):
    t = x.shape[0]
    cap = EC_FACTOR * t // N_EXPERTS
    tok, gate = expert_choice(aff[:, :N_EXPERTS].T, cap=cap)
    return moe_apply(f, x, tok, gate, gate_f, wg, wu, wd, layer=layer, tm=min(tm, cap))


def kernel(x, c, ctx, c_ctx, ada_w, ada_b, norm_mix, norm_ffn, mla_w_in, mla_q_a_gain, mla_kv_a_gain,
           mla_w_qb, mla_w_kvb, mla_q_gain, mla_k_gain, mla_w_o, na_w_qkv, na_q_gain, na_k_gain, na_rpb,
           na_w_o, router_w, moe_w_gate, moe_w_up, moe_w_down):
    n = x.shape[1]
    n_ctx = ctx.shape[1]
    d = D_MODEL
    x_lat = x[0]
    x_ctx = ctx[0]

    mods = ada_modulation(jnp.stack([c[0], c_ctx]), ada_w, ada_b)

    def mod(layer, r, k):
        return mods[layer, r:r + 1, k * d:(k + 1) * d]

    row = lambda v: v.reshape(1, -1)
    rw_hi = [_pad_lanes(router_w[i]).astype(BF16) for i in range(2)]
    rw_lo = [(_pad_lanes(router_w[i]) - rw_hi[i].astype(F32)).astype(BF16) for i in range(2)]
    wg = moe_w_gate.astype(BF16)
    wu = moe_w_up.astype(BF16)
    wd = moe_w_down.astype(BF16)

    w_in = mla_w_in[0]
    kpe_w = w_in[:, Q_LORA + KV_LORA:]
    w_in_ext = jnp.concatenate(
        [w_in[:, :Q_LORA + KV_LORA], _pad_lanes(kpe_w), _pad_lanes(_swap_rot_halves(kpe_w))], axis=1).astype(BF16)
    wq = mla_w_qb[0].reshape(Q_LORA, MLA_HEADS, QK_HEAD)
    wq_ext = jnp.concatenate(
        [wq[..., :QK_NOPE], _pad_lanes(wq[..., QK_NOPE:]), _pad_lanes(_swap_rot_halves(wq[..., QK_NOPE:]))],
        axis=-1).reshape(Q_LORA, -1).astype(BF16)
    wkv = mla_w_kvb[0].astype(BF16)
    gq = _gain_rows(mla_q_gain[0])
    gk = _gain_rows(mla_k_gain[0])
    rope_c, rope_s = _rope_tables(n)
    ones_c = _pad_lanes(jnp.ones((n_ctx, QK_ROPE), F32))
    zeros_s = jnp.zeros((n_ctx, LANES), F32)
    no_gain = jnp.ones((1, w_in_ext.shape[1]), F32)

    a_lat = norm_mod_proj(x_lat, row(norm_mix[0]), mod(0, 0, 1), mod(0, 0, 0), w_in_ext, no_gain,
                          tm=512, tn=w_in_ext.shape[1], out_dtype=F32)
    a_ctx = norm_mod_proj(x_ctx, row(norm_mix[0]), mod(0, 1, 1), mod(0, 1, 0), w_in_ext, no_gain,
                          tm=n_ctx, tn=w_in_ext.shape[1], out_dtype=F32)
    q_l, k_l, v_l = mla_qkv(a_lat, row(mla_q_a_gain[0]), row(mla_kv_a_gain[0]), wq_ext, wkv, gq, gk,
                            rope_c, rope_s, tm=256)
    q_c, k_c, v_c = mla_qkv(a_ctx, row(mla_q_a_gain[0]), row(mla_kv_a_gain[0]), wq_ext, wkv, gq, gk,
                            ones_c, zeros_s, tm=n_ctx)
    o_lat = mla_attention(q_l, k_c, v_c, k_l, v_l, tq=1024, tk=1024)
    o_ctx = mla_attention(q_c, k_c, v_c, tq=n_ctx)

    w_o = mla_w_o[0].astype(BF16)
    x_lat, f_lat, aff_lat = outproj_ffnprep(o_lat, w_o, x_lat, mod(0, 0, 2), row(norm_ffn[0]), mod(0, 0, 4),
                                            mod(0, 0, 3), rw_hi[0], rw_lo[0], tm=512)
    x_ctx, f_ctx, aff_ctx = outproj_ffnprep(o_ctx, w_o, x_ctx, mod(0, 1, 2), row(norm_ffn[0]), mod(0, 1, 4),
                                            mod(0, 1, 3), rw_hi[0], rw_lo[0], tm=n_ctx)
    x_lat = _moe_layer(f_lat, x_lat, aff_lat, mod(0, 0, 5), wg, wu, wd, layer=0, tm=256)
    x_ctx = _moe_layer(f_ctx, x_ctx, aff_ctx, mod(0, 1, 5), wg, wu, wd, layer=0, tm=256)

    w_qkv = na_w_qkv[0].astype(BF16)
    qkv_gain = jnp.concatenate([jnp.tile(na_q_gain[0], NA_HEADS) * (NA_HEAD_DIM ** -0.5 * LOG2_E),
                                jnp.tile(na_k_gain[0], NA_HEADS), jnp.ones((d,), F32)]).reshape(1, -1)
    qkv_l = norm_mod_proj(x_lat, row(norm_mix[1]), mod(1, 0, 1), mod(1, 0, 0), w_qkv, qkv_gain,
                          tm=1024, tn=1024, out_dtype=BF16, n_norm_blocks=4)
    qkv_c = norm_mod_proj(x_ctx, row(norm_mix[1]), mod(1, 1, 1), mod(1, 1, 0), w_qkv, qkv_gain,
                          tm=n_ctx, tn=1024, out_dtype=BF16, n_norm_blocks=4)
    o_lat = na_attention(qkv_l, qkv_c, na_rpb[0])
    x_lat, f_lat, aff_lat = outproj_ffnprep(o_lat, na_w_o[0].astype(BF16), x_lat, mod(1, 0, 2), row(norm_ffn[1]),
                                            mod(1, 0, 4), mod(1, 0, 3), rw_hi[1], rw_lo[1], tm=512)
    x_lat = _moe_layer(f_lat, x_lat, aff_lat, mod(1, 0, 5), wg, wu, wd, layer=1, tm=256)
    return x_lat[None]
```

```python
import functools
import math

import jax
import jax.numpy as jnp
from jax import lax
from jax.experimental import pallas as pl
from jax.experimental.pallas import tpu as pltpu

F32 = jnp.float32
BF16 = jnp.bfloat16

D_MODEL = 2048
GRID_W = 64
EPS = 1e-6
ROPE_THETA = 10000.0

MLA_HEADS = 16
Q_LORA = 512
KV_LORA = 512
QK_NOPE = 128
QK_ROPE = 64
QK_HEAD = QK_NOPE + QK_ROPE
V_DIM = 128
MLA_QK_PAD = 256

NA_HEADS = 16
NA_HEAD_DIM = 128
NA_KH = 8
NA_KW = 16

N_EXPERTS = 16
EXPERT_FF = 1408
EC_FACTOR = 2

LANES = 128
SUBLANES = 8
NEG_BIG = -1e30
LOG2_E = math.log2(math.e)

VMEM_LIMIT = 56 * 1024 * 1024


def _cparams(sem):
    return pltpu.CompilerParams(dimension_semantics=sem, vmem_limit_bytes=VMEM_LIMIT)


def _ada_kernel(c_ref, w_ref, b_ref, o_ref, *, tn):
    for r in range(2):
        c = c_ref[r]
        act = c * (1.0 / (1.0 + jnp.exp(-c)))
        for k in range(tn // LANES):
            sl = slice(k * LANES, (k + 1) * LANES)
            o_ref[0, r:r + 1, sl] = jnp.sum(act * w_ref[0, :, sl], axis=0, keepdims=True) + b_ref[0, :, sl]


def ada_modulation(cvec, ada_w, ada_b, *, tn=512):
    depth, d, n = ada_w.shape
    c_rep = jnp.broadcast_to(cvec[:, :, None], (2, d, LANES))
    return pl.pallas_call(
        functools.partial(_ada_kernel, tn=tn),
        out_shape=jax.ShapeDtypeStruct((depth, 2, n), F32),
        grid=(depth, n // tn),
        in_specs=[
            pl.BlockSpec((2, d, LANES), lambda l, j: (0, 0, 0)),
            pl.BlockSpec((1, d, tn), lambda l, j: (l, 0, j)),
            pl.BlockSpec((1, 1, tn), lambda l, j: (l, 0, j)),
        ],
        out_specs=pl.BlockSpec((1, 2, tn), lambda l, j: (l, 0, j)),
        compiler_params=_cparams(("arbitrary", "arbitrary")),
        name="ada_modulation",
    )(c_rep, ada_w, ada_b.reshape(depth, 1, n))


def _proj_kernel(x_ref, g_ref, sc_ref, sh_ref, w_ref, cg_ref, o_ref, h_ref, *, tn, n_norm_blocks, head):
    j = pl.program_id(1)

    @pl.when(j == 0)
    def _():
        x = x_ref[...]
        y = x * lax.rsqrt(jnp.mean(x * x, axis=-1, keepdims=True) + EPS) * g_ref[...]
        h_ref[...] = (y * (1.0 + sc_ref[...]) + sh_ref[...]).astype(BF16)

    acc = jnp.dot(h_ref[...], w_ref[...], preferred_element_type=F32)
    if n_norm_blocks == 0:
        o_ref[...] = acc.astype(o_ref.dtype)
    else:
        @pl.when(j < n_norm_blocks)
        def _():
            for c in range(tn // head):
                sl = slice(c * head, (c + 1) * head)
                y = acc[:, sl]
                r = lax.rsqrt(jnp.mean(y * y, axis=-1, keepdims=True) + EPS)
                o_ref[:, sl] = (y * r * cg_ref[:, sl]).astype(o_ref.dtype)

        @pl.when(j >= n_norm_blocks)
        def _():
            o_ref[...] = acc.astype(o_ref.dtype)


def norm_mod_proj(x, gain, scale, shift, w, col_gain, *, tm, tn, out_dtype, n_norm_blocks=0, head=LANES):
    m = x.shape[0]
    d, n = w.shape
    return pl.pallas_call(
        functools.partial(_proj_kernel, tn=tn, n_norm_blocks=n_norm_blocks, head=head),
        out_shape=jax.ShapeDtypeStruct((m, n), out_dtype),
        grid=(m // tm, n // tn),
        in_specs=[
            pl.BlockSpec((tm, d), lambda i, j: (i, 0)),
            pl.BlockSpec((1, d), lambda i, j: (0, 0)),
            pl.BlockSpec((1, d), lambda i, j: (0, 0)),
            pl.BlockSpec((1, d), lambda i, j: (0, 0)),
            pl.BlockSpec((d, tn), lambda i, j: (0, j)),
            pl.BlockSpec((1, tn), lambda i, j: (0, j)),
        ],
        out_specs=pl.BlockSpec((tm, tn), lambda i, j: (i, j)),
        scratch_shapes=[pltpu.VMEM((tm, d), BF16)],
        compiler_params=_cparams(("arbitrary", "arbitrary")),
        name="norm_mod_proj",
    )(x, gain, scale, shift, w, col_gain)


def _mla_qkv_kernel(a_ref, qag_ref, kvag_ref, wq_ref, wkv_ref, gq_ref, gk_ref, rc_ref, rs_ref,
                    q_ref, k_ref, v_ref, *, heads, q_scale):
    a = a_ref[...]
    qc = a[:, :Q_LORA]
    kvc = a[:, Q_LORA:Q_LORA + KV_LORA]
    kpe1 = a[:, Q_LORA + KV_LORA:Q_LORA + KV_LORA + LANES]
    kpe2 = a[:, Q_LORA + KV_LORA + LANES:]

    def rms(x, g):
        return x * lax.rsqrt(jnp.mean(x * x, axis=-1, keepdims=True) + EPS) * g

    qn = rms(qc, qag_ref[...]).astype(BF16)
    kvn = rms(kvc, kvag_ref[...]).astype(BF16)
    rc = rc_ref[...]
    rs = rs_ref[...]
    gq_nope, gq1, gq2 = gq_ref[0:1, :], gq_ref[1:2, :], gq_ref[2:3, :]
    gk_nope, gk1, gk2 = gk_ref[0:1, :], gk_ref[1:2, :], gk_ref[2:3, :]
    kpe_ss = jnp.sum(kpe1 * kpe1, axis=-1, keepdims=True)
    kpe_rot = kpe1 * gk1 * rc + kpe2 * gk2 * rs
    qw = QK_NOPE + 2 * LANES
    for h in range(heads):
        qh = jnp.dot(qn, wq_ref[:, h * qw:(h + 1) * qw], preferred_element_type=F32)
        qa, q1, q2 = qh[:, :QK_NOPE], qh[:, QK_NOPE:QK_NOPE + LANES], qh[:, QK_NOPE + LANES:]
        ss = jnp.sum(qa * qa, axis=-1, keepdims=True) + jnp.sum(q1 * q1, axis=-1, keepdims=True)
        r = lax.rsqrt(ss * (1.0 / QK_HEAD) + EPS) * q_scale
        q_ref[:, h * MLA_QK_PAD:h * MLA_QK_PAD + QK_NOPE] = (qa * r * gq_nope).astype(BF16)
        q_ref[:, h * MLA_QK_PAD + QK_NOPE:(h + 1) * MLA_QK_PAD] = (
            (q1 * gq1 * rc + q2 * gq2 * rs) * r).astype(BF16)
        kvh = jnp.dot(kvn, wkv_ref[:, h * (QK_NOPE + V_DIM):(h + 1) * (QK_NOPE + V_DIM)],
                      preferred_element_type=F32)
        kn, vv = kvh[:, :QK_NOPE], kvh[:, QK_NOPE:]
        rk = lax.rsqrt((jnp.sum(kn * kn, axis=-1, keepdims=True) + kpe_ss) * (1.0 / QK_HEAD) + EPS)
        k_ref[:, h * MLA_QK_PAD:h * MLA_QK_PAD + QK_NOPE] = (kn * rk * gk_nope).astype(BF16)
        k_ref[:, h * MLA_QK_PAD + QK_NOPE:(h + 1) * MLA_QK_PAD] = (kpe_rot * rk).astype(BF16)
        v_ref[:, h * V_DIM:(h + 1) * V_DIM] = vv.astype(BF16)


def mla_qkv(a, q_a_gain, kv_a_gain, wq_ext, wkv, gq, gk, rope_c, rope_s, *, tm, heads=MLA_HEADS):
    m, aw = a.shape
    full = lambda i: (0, 0)
    return pl.pallas_call(
        functools.partial(_mla_qkv_kernel, heads=heads, q_scale=QK_HEAD ** -0.5 * LOG2_E),
        out_shape=(
            jax.ShapeDtypeStruct((m, heads * MLA_QK_PAD), BF16),
            jax.ShapeDtypeStruct((m, heads * MLA_QK_PAD), BF16),
            jax.ShapeDtypeStruct((m, heads * V_DIM), BF16),
        ),
        grid=(m // tm,),
        in_specs=[
            pl.BlockSpec((tm, aw), lambda i: (i, 0)),
            pl.BlockSpec((1, Q_LORA), full),
            pl.BlockSpec((1, KV_LORA), full),
            pl.BlockSpec(wq_ext.shape, full),
            pl.BlockSpec(wkv.shape, full),
            pl.BlockSpec((8, LANES), full),
            pl.BlockSpec((8, LANES), full),
            pl.BlockSpec((tm, LANES), lambda i: (i, 0)),
            pl.BlockSpec((tm, LANES), lambda i: (i, 0)),
        ],
        out_specs=(
            pl.BlockSpec((tm, heads * MLA_QK_PAD), lambda i: (i, 0)),
            pl.BlockSpec((tm, heads * MLA_QK_PAD), lambda i: (i, 0)),
            pl.BlockSpec((tm, heads * V_DIM), lambda i: (i, 0)),
        ),
        compiler_params=_cparams(("arbitrary",)),
        name="mla_qkv",
    )(a, q_a_gain, kv_a_gain, wq_ext, wkv, gq, gk, rope_c, rope_s)


def _nt_dot(a, b):
    return lax.dot_general(a, b, (((1,), (1,)), ((), ())), preferred_element_type=F32)


ATTN_UNROLL = 4


def _mla_attn_kernel(*refs, tk, n_lat):
    if n_lat:
        q_ref, kc_ref, vc_ref, kl_ref, vl_ref, o_ref = refs
    else:
        q_ref, kc_ref, vc_ref, o_ref = refs
    tq = q_ref.shape[0]

    def step(k, v, m, l, acc):
        s = _nt_dot(q_ref[...], k)
        m_new = jnp.maximum(m, jnp.max(s, axis=-1, keepdims=True))
        p = jnp.exp2(s - m_new)
        alpha = jnp.exp2(m - m_new)
        l_new = alpha * l + jnp.sum(p, axis=-1, keepdims=True)
        acc_new = alpha * acc + jnp.dot(p.astype(BF16), v, preferred_element_type=F32)
        return m_new, l_new, acc_new

    m0 = jnp.full((tq, 1), NEG_BIG, F32)
    l0 = jnp.zeros((tq, 1), F32)
    acc0 = jnp.zeros((tq, V_DIM), F32)
    carry = step(kc_ref[...], vc_ref[...], m0, l0, acc0)
    if n_lat:
        def body(i, carry):
            start = pl.multiple_of(i * tk, tk)
            return step(kl_ref[pl.ds(start, tk), :], vl_ref[pl.ds(start, tk), :], *carry)

        carry = lax.fori_loop(0, n_lat // tk, body, carry, unroll=ATTN_UNROLL)
    _, l, acc = carry
    o_ref[...] = (acc * (1.0 / l)).astype(o_ref.dtype)


def mla_attention(q, k_ctx, v_ctx, k_lat=None, v_lat=None, *, tq, tk=512, heads=MLA_HEADS):
    nq = q.shape[0]
    n_ctx = k_ctx.shape[0]
    n_lat = 0 if k_lat is None else k_lat.shape[0]
    in_specs = [
        pl.BlockSpec((tq, MLA_QK_PAD), lambda h, i: (i, h)),
        pl.BlockSpec((n_ctx, MLA_QK_PAD), lambda h, i: (0, h)),
        pl.BlockSpec((n_ctx, V_DIM), lambda h, i: (0, h)),
    ]
    args = [q, k_ctx, v_ctx]
    if n_lat:
        in_specs += [
            pl.BlockSpec((n_lat, MLA_QK_PAD), lambda h, i: (0, h)),
            pl.BlockSpec((n_lat, V_DIM), lambda h, i: (0, h)),
        ]
        args += [k_lat, v_lat]
    return pl.pallas_call(
        functools.partial(_mla_attn_kernel, tk=tk, n_lat=n_lat),
        out_shape=jax.ShapeDtypeStruct((nq, heads * V_DIM), BF16),
        grid=(heads, nq // tq),
        in_specs=in_specs,
        out_specs=pl.BlockSpec((tq, V_DIM), lambda h, i: (i, h)),
        compiler_params=_cparams(("arbitrary", "arbitrary")),
        name="mla_attention",
    )(*args)


def _na_build_bias(rpb_ref, bias_sc, *, rb, span_rows, rows):
    nblk = rows // rb
    qc = lax.broadcasted_iota(jnp.int32, (GRID_W, LANES), 0)
    lane = lax.broadcasted_iota(jnp.int32, (GRID_W, LANES), 1)
    kc = lane & (GRID_W - 1)
    col_start = jnp.clip(qc - NA_KW // 2, 0, GRID_W - NA_KW)
    col_ok = (kc >= col_start) & (kc < col_start + NA_KW)
    first = lane < GRID_W
    neg = jnp.full((GRID_W, LANES), NEG_BIG, F32)

    def toeplitz(dr, shift):
        w = jnp.broadcast_to(rpb_ref[0, dr + NA_KH - 1:dr + NA_KH, :] * LOG2_E, (GRID_W, LANES))
        return pltpu.roll(w, shift, 1, stride=1, stride_axis=0)

    pairs = {}

    def pair(dr, ok0, ok1):
        key = (dr, ok0, ok1)
        if key not in pairs:
            a = toeplitz(dr, LANES - (NA_KW - 1)) if ok0 else neg
            b = toeplitz(dr + 1, GRID_W - (NA_KW - 1)) if ok1 else neg
            pairs[key] = jnp.where(col_ok, jnp.where(first, a, b), NEG_BIG)
        return pairs[key]

    for v, blk in enumerate((0, 1, nblk - 1)):
        kr0 = min(max(blk * rb - NA_KH // 2, 0), rows - span_rows)
        for qr in range(rb):
            r = blk * rb + qr
            r0 = min(max(r - NA_KH // 2, 0), rows - NA_KH)
            for m in range(span_rows // 2):
                ka = kr0 + 2 * m
                ok0 = r0 <= ka < r0 + NA_KH
                ok1 = r0 <= ka + 1 < r0 + NA_KH
                blk_val = pair(ka - r, ok0, ok1) if (ok0 or ok1) else neg
                bias_sc[v, qr * GRID_W:(qr + 1) * GRID_W, m * LANES:(m + 1) * LANES] = blk_val


def _na_attn_kernel(q_ref, k_ref, v_ref, kc_ref, vc_ref, rpb_ref, o_ref, bias_sc, *, rb, span_rows, rows, sub):
    nblk = rows // rb
    tq = rb * GRID_W

    @pl.when(pl.program_id(1) == 0)
    def _():
        _na_build_bias(rpb_ref, bias_sc, rb=rb, span_rows=span_rows, rows=rows)

    span = span_rows * GRID_W
    for u in range(sub):
        i = pl.program_id(1) * sub + u
        kr0 = jnp.clip(i * rb - NA_KH // 2, 0, rows - span_rows)
        start = pl.multiple_of(kr0 * GRID_W, (NA_KH // 2) * GRID_W)
        variant = jnp.where(i == 0, 0, jnp.where(i == nblk - 1, 2, 1))
        q = q_ref[u * tq:(u + 1) * tq, :]
        kw = k_ref[pl.ds(start, span), :]
        vw = v_ref[pl.ds(start, span), :]
        s_win = _nt_dot(q, kw) + bias_sc[variant]
        s_ctx = _nt_dot(q, kc_ref[...])
        m = jnp.maximum(jnp.max(s_win, axis=-1, keepdims=True), jnp.max(s_ctx, axis=-1, keepdims=True))
        p_win = jnp.exp2(s_win - m)
        p_ctx = jnp.exp2(s_ctx - m)
        l = jnp.sum(p_win, axis=-1, keepdims=True) + jnp.sum(p_ctx, axis=-1, keepdims=True)
        o = (jnp.dot(p_win.astype(BF16), vw, preferred_element_type=F32)
             + jnp.dot(p_ctx.astype(BF16), vc_ref[...], preferred_element_type=F32))
        o_ref[u * tq:(u + 1) * tq, :] = (o * (1.0 / l)).astype(o_ref.dtype)


def na_attention(qkv, qkv_ctx, rpb, *, rb=8, sub=8, heads=NA_HEADS):
    n = qkv.shape[0]
    n_ctx = qkv_ctx.shape[0]
    rows = n // GRID_W
    span_rows = rb + NA_KH
    nblk = rows // rb
    sub = math.gcd(sub, nblk)
    tq = rb * GRID_W
    rpb_pad = jnp.pad(rpb, ((0, 0), (0, 2 * NA_KH - rpb.shape[1]), (0, LANES - rpb.shape[2])))
    return pl.pallas_call(
        functools.partial(_na_attn_kernel, rb=rb, span_rows=span_rows, rows=rows, sub=sub),
        out_shape=jax.ShapeDtypeStruct((n, heads * NA_HEAD_DIM), BF16),
        grid=(heads, nblk // sub),
        in_specs=[
            pl.BlockSpec((sub * tq, NA_HEAD_DIM), lambda h, i: (i, h)),
            pl.BlockSpec((n, NA_HEAD_DIM), lambda h, i: (0, heads + h)),
            pl.BlockSpec((n, NA_HEAD_DIM), lambda h, i: (0, 2 * heads + h)),
            pl.BlockSpec((n_ctx, NA_HEAD_DIM), lambda h, i: (0, heads + h)),
            pl.BlockSpec((n_ctx, NA_HEAD_DIM), lambda h, i: (0, 2 * heads + h)),
            pl.BlockSpec((1, 2 * NA_KH, LANES), lambda h, i: (h, 0, 0)),
        ],
        out_specs=pl.BlockSpec((sub * tq, NA_HEAD_DIM), lambda h, i: (i, h)),
        scratch_shapes=[pltpu.VMEM((3, tq, span_rows * GRID_W), F32)],
        compiler_params=_cparams(("arbitrary", "arbitrary")),
        name="na_attention",
    )(qkv, qkv, qkv, qkv_ctx, qkv_ctx, rpb_pad)


def _outproj_kernel(o_ref, w_ref, x_ref, ga_ref, g_ref, sc_ref, sh_ref, rwh_ref, rwl_ref,
                    xf_ref, aff_ref, *, n_experts):
    d = x_ref.shape[1]
    acc = jnp.dot(o_ref[...], w_ref[...], preferred_element_type=F32)
    xn = x_ref[...] + ga_ref[...] * acc
    xf_ref[:, :d] = xn
    y = xn * lax.rsqrt(jnp.mean(xn * xn, axis=-1, keepdims=True) + EPS) * g_ref[...]
    f = y * (1.0 + sc_ref[...]) + sh_ref[...]
    xf_ref[:, d:] = f
    f_hi = f.astype(BF16)
    f_lo = (f - f_hi.astype(F32)).astype(BF16)
    logits = (jnp.dot(f_hi, rwh_ref[...], preferred_element_type=F32)
              + jnp.dot(f_lo, rwh_ref[...], preferred_element_type=F32)
              + jnp.dot(f_hi, rwl_ref[...], preferred_element_type=F32))
    lane = lax.broadcasted_iota(jnp.int32, logits.shape, 1)
    logits = jnp.where(lane < n_experts, logits, NEG_BIG)
    e = jnp.exp(logits - jnp.max(logits, axis=-1, keepdims=True))
    aff_ref[...] = e * (1.0 / jnp.sum(e, axis=-1, keepdims=True))


def outproj_ffnprep(o, w_o, x, gate_a, gain, scale, shift, rw_hi, rw_lo, *, tm, n_experts=N_EXPERTS):
    m = x.shape[0]
    k, d = w_o.shape
    row = lambda i: (i, 0)
    full = lambda i: (0, 0)
    return pl.pallas_call(
        functools.partial(_outproj_kernel, n_experts=n_experts),
        out_shape=(
            jax.ShapeDtypeStruct((m, 2 * d), F32),
            jax.ShapeDtypeStruct((m, LANES), F32),
        ),
        grid=(m // tm,),
        in_specs=[
            pl.BlockSpec((tm, k), row),
            pl.BlockSpec((k, d), full),
            pl.BlockSpec((tm, d), row),
            pl.BlockSpec((1, d), full),
            pl.BlockSpec((1, d), full),
            pl.BlockSpec((1, d), full),
            pl.BlockSpec((1, d), full),
            pl.BlockSpec((d, LANES), full),
            pl.BlockSpec((d, LANES), full),
        ],
        out_specs=(
            pl.BlockSpec((tm, 2 * d), row),
            pl.BlockSpec((tm, LANES), row),
        ),
        compiler_params=_cparams(("arbitrary",)),
        name="outproj_ffnprep",
    )(o, w_o, x, gate_a, gain, scale, shift, rw_hi, rw_lo)


def _prefix_rank(sel_bf, upper_incl, lower_strict):
    w = jnp.dot(sel_bf, upper_incl, preferred_element_type=F32)
    row_tot = jnp.broadcast_to(w[:, LANES - 1:LANES], w.shape)
    row_off = jnp.dot(lower_strict, row_tot.astype(BF16), preferred_element_type=F32)
    return w, row_tot, row_off


def _topk_kernel(a_ref, tok_ref, gate_ref, *, cap, jc):
    a = a_ref[0]
    tb = a.shape[0]
    ri = lax.broadcasted_iota(jnp.int32, (tb, LANES), 0)
    ci = lax.broadcasted_iota(jnp.int32, (tb, LANES), 1)
    upper_incl = (ri <= ci).astype(BF16)
    lower_strict = (ci < ri).astype(BF16)

    def count(mask):
        c = jnp.sum(mask.astype(F32), axis=1, keepdims=True)
        return jnp.sum(c, axis=0, keepdims=True)

    thr = jnp.zeros((1, 1), jnp.int32)
    for bit in range(30, -1, -1):
        cand = thr | jnp.int32(1 << bit)
        candf = lax.bitcast_convert_type(cand, F32)
        thr = jnp.where(count(a >= candf) >= cap, cand, thr)
    thrf = lax.bitcast_convert_type(thr, F32)
    gt = a > thrf
    eq = a == thrf
    need = cap - count(gt)
    w_eq, _, off_eq = _prefix_rank(eq.astype(BF16), upper_incl, lower_strict)
    sel = gt | (eq & (w_eq + off_eq <= need))
    sel_bf = sel.astype(BF16)
    w, row_tot, row_off = _prefix_rank(sel_bf, upper_incl, lower_strict)
    tot_t = row_tot.T
    incl_t = jnp.dot(tot_t.astype(BF16), upper_incl, preferred_element_type=F32)
    excl_t = incl_t - tot_t
    excl_row = excl_t[0:1, :]
    incl_row = incl_t[0:1, :]
    w_bf = w.astype(BF16)
    a1 = a.astype(BF16)
    a2 = (a - a1.astype(F32)).astype(BF16)
    a3 = (a - a1.astype(F32) - a2.astype(F32)).astype(BF16)
    lane_f = lax.broadcasted_iota(jnp.int32, (jc, LANES), 1).astype(F32)
    for c in range(cap // jc):
        slot = (lax.broadcasted_iota(jnp.int32, (jc, 1), 0) + c * jc).astype(F32)
        in_row = (excl_row <= slot) & (slot < incl_row)
        in_row_f = in_row.astype(F32)
        in_row_bf = in_row.astype(BF16)
        w_g = jnp.dot(in_row_bf, w_bf, preferred_element_type=F32)
        s_g = jnp.dot(in_row_bf, sel_bf, preferred_element_type=F32)
        off_g = jnp.sum(in_row_f * excl_row, axis=1, keepdims=True)
        hit = (s_g > 0.5) & (w_g + off_g == slot + 1.0)
        hit_f = hit.astype(F32)
        row_id = jnp.sum(in_row_f * lane_f, axis=1, keepdims=True)
        lane_id = jnp.sum(hit_f * lane_f, axis=1, keepdims=True)
        a_g = (jnp.dot(in_row_bf, a1, preferred_element_type=F32)
               + jnp.dot(in_row_bf, a2, preferred_element_type=F32)
               + jnp.dot(in_row_bf, a3, preferred_element_type=F32))
        tok_ref[0, c * jc:(c + 1) * jc, :] = (row_id * float(LANES) + lane_id).astype(jnp.int32)
        gate_ref[0, c * jc:(c + 1) * jc, :] = jnp.sum(hit_f * a_g, axis=1, keepdims=True)


def expert_choice(aff_t, *, cap):
    e, t = aff_t.shape
    t_pad = LANES * LANES
    a = jnp.pad(aff_t, ((0, 0), (0, t_pad - t)), constant_values=-1.0).reshape(e, LANES, LANES)
    jc = min(cap, 256)
    tok, gate = pl.pallas_call(
        functools.partial(_topk_kernel, cap=cap, jc=jc),
        out_shape=(
            jax.ShapeDtypeStruct((e, cap, 1), jnp.int32),
            jax.ShapeDtypeStruct((e, cap, 1), F32),
        ),
        grid=(e,),
        in_specs=[pl.BlockSpec((1, LANES, LANES), lambda i: (i, 0, 0))],
        out_specs=(
            pl.BlockSpec((1, cap, 1), lambda i: (i, 0, 0)),
            pl.BlockSpec((1, cap, 1), lambda i: (i, 0, 0)),
        ),
        compiler_params=_cparams(("arbitrary",)),
        name="expert_choice",
    )(a)
    return tok.reshape(e, cap), gate


def _moe_kernel(idx_hbm, xf_hbm, g_ref, gf_ref, wg_ref, wu_ref, wd_ref, o_hbm,
                idx_s, xs, sem_idx, sem_f, sem_x, sem_o, *, tm, chunks, n_steps):
    del xf_hbm
    step = pl.program_id(0) * chunks + pl.program_id(1)
    stride = idx_hbm.shape[1]
    cur, nxt, prv = step % 3, (step + 1) % 3, (step + 2) % 3
    d = xs.shape[-1] // 2

    def idx_copy(s, slot):
        return pltpu.make_async_copy(
            idx_hbm.at[s], idx_s.at[pl.ds(pl.multiple_of(slot * stride, stride), stride)], sem_idx)

    def row_in(g, u, slot):
        tok = idx_s[slot * stride + g * SUBLANES + u]
        return pltpu.make_async_copy(o_hbm.at[pl.ds(tok, 1)], xs.at[slot, g, pl.ds(u, 1)], sem_f.at[slot])

    def f_in(g, u, slot):
        tok = idx_s[slot * stride + g * SUBLANES + u]
        return pltpu.make_async_copy(o_hbm.at[pl.ds(tok, 1), pl.ds(d, d)],
                                     xs.at[slot, g, pl.ds(u, 1), pl.ds(d, d)], sem_f.at[slot])

    def x_refresh(g, u):
        tok = idx_s[cur * stride + g * SUBLANES + u]
        return pltpu.make_async_copy(o_hbm.at[pl.ds(tok, 1), pl.ds(0, d)],
                                     xs.at[cur, g, pl.ds(u, 1), pl.ds(0, d)], sem_x)

    def x_row_out(g, u, slot):
        tok = idx_s[slot * stride + g * SUBLANES + u]
        return pltpu.make_async_copy(xs.at[slot, g, pl.ds(u, 1), pl.ds(0, d)],
                                     o_hbm.at[pl.ds(tok, 1), pl.ds(0, d)], sem_o)

    def rows(fn):
        def body(g, c):
            for u in range(SUBLANES):
                fn(g, u)
            return c
        lax.fori_loop(0, tm // SUBLANES, body, 0)

    @pl.when(step == 0)
    def _():
        idx_copy(0, 0).start()
        idx_copy(0, 0).wait()
        rows(lambda g, u: row_in(g, u, 0).start(priority=u % 2))

    has_next = step + 1 < n_steps
    last_chunk = pl.program_id(1) == chunks - 1
    new_expert = (step > 0) & (pl.program_id(1) == 0)
    same_expert = (step > 0) & (pl.program_id(1) != 0)

    @pl.when(has_next)
    def _():
        idx_copy(step + 1, nxt).start()
        idx_copy(step + 1, nxt).wait()

    @pl.when(has_next & last_chunk)
    def _():
        rows(lambda g, u: f_in(g, u, nxt).start(priority=u % 2))

    @pl.when(has_next & jnp.logical_not(last_chunk))
    def _():
        rows(lambda g, u: row_in(g, u, nxt).start(priority=u % 2))

    @pl.when(new_expert)
    def _():
        rows(lambda g, u: x_row_out(g, u, prv).wait())
        rows(lambda g, u: x_refresh(g, u).start(priority=u % 2))
        rows(lambda g, u: f_in(g, u, cur).wait())

    @pl.when(jnp.logical_not(new_expert))
    def _():
        rows(lambda g, u: row_in(g, u, cur).wait())

    xb = xs[cur, :, :, d:].reshape(tm, d).astype(BF16)
    hg = jnp.dot(xb, wg_ref[0, 0], preferred_element_type=F32)
    hu = jnp.dot(xb, wu_ref[0, 0], preferred_element_type=F32)
    hid = (hg * (1.0 / (1.0 + jnp.exp(-hg)))) * hu
    y = jnp.dot(hid.astype(BF16), wd_ref[0, 0], preferred_element_type=F32)
    upd = gf_ref[...] * (y * g_ref[...])

    @pl.when(new_expert)
    def _():
        rows(lambda g, u: x_refresh(g, u).wait())

    xs[cur, :, :, :d] = xs[cur, :, :, :d] + upd.reshape(tm // SUBLANES, SUBLANES, d)

    @pl.when(same_expert)
    def _():
        rows(lambda g, u: x_row_out(g, u, prv).wait())

    rows(lambda g, u: x_row_out(g, u, cur).start(priority=u % 2))

    @pl.when(step == n_steps - 1)
    def _():
        rows(lambda g, u: x_row_out(g, u, cur).wait())


def moe_apply(xf, tok, gate, gate_f, wg, wu, wd, *, layer, tm):
    t = xf.shape[0]
    d = xf.shape[1] // 2
    e, cap = tok.shape
    chunks = cap // tm
    ff = wg.shape[3]
    stride = max(tm, LANES)
    tok_rows = jnp.pad(tok.reshape(e * chunks, tm), ((0, 0), (0, stride - tm)))
    return pl.pallas_call(
        functools.partial(_moe_kernel, tm=tm, chunks=chunks, n_steps=e * chunks),
        out_shape=jax.ShapeDtypeStruct((t, 2 * d), F32),
        grid=(e, chunks),
        in_specs=[
            pl.BlockSpec(memory_space=pl.ANY),
            pl.BlockSpec(memory_space=pl.ANY),
            pl.BlockSpec((tm, 1), lambda i, c: (i * chunks + c, 0)),
            pl.BlockSpec((1, d), lambda i, c: (0, 0)),
            pl.BlockSpec((1, 1, d, ff), lambda i, c: (layer, i, 0, 0)),
            pl.BlockSpec((1, 1, d, ff), lambda i, c: (layer, i, 0, 0)),
            pl.BlockSpec((1, 1, ff, d), lambda i, c: (layer, i, 0, 0)),
        ],
        out_specs=pl.BlockSpec(memory_space=pl.ANY),
        scratch_shapes=[
            pltpu.SMEM((3 * stride,), jnp.int32),
            pltpu.VMEM((3, tm // SUBLANES, SUBLANES, 2 * d), F32),
            pltpu.SemaphoreType.DMA,
            pltpu.SemaphoreType.DMA((3,)),
            pltpu.SemaphoreType.DMA,
            pltpu.SemaphoreType.DMA,
        ],
        input_output_aliases={1: 0},
        compiler_params=_cparams(("arbitrary", "arbitrary")),
        name="moe_apply",
    )(tok_rows, xf, gate.reshape(e * cap, 1), gate_f, wg, wu, wd)


def _swap_rot_halves(w):
    q = QK_ROPE // 4
    return jnp.concatenate([w[..., q:2 * q], w[..., :q], w[..., 3 * q:], w[..., 2 * q:3 * q]], axis=-1)


def _pad_lanes(w):
    return jnp.pad(w, [(0, 0)] * (w.ndim - 1) + [(0, LANES - w.shape[-1])])


def _rope_tables(n):
    half = QK_ROPE // 2
    inv = ROPE_THETA ** (-jnp.arange(0, half, 2, dtype=F32) / half)
    t = jnp.arange(n)
    ang_r = (t // GRID_W).astype(F32)[:, None] * inv[None, :]
    ang_c = (t % GRID_W).astype(F32)[:, None] * inv[None, :]
    cr, sr, cc, sc = jnp.cos(ang_r), jnp.sin(ang_r), jnp.cos(ang_c), jnp.sin(ang_c)
    return (_pad_lanes(jnp.concatenate([cr, cr, cc, cc], axis=-1)),
            _pad_lanes(jnp.concatenate([-sr, sr, -sc, sc], axis=-1)))


def _gain_rows(g):
    pe = g[QK_NOPE:]
    rows = jnp.stack([g[:QK_NOPE], _pad_lanes(pe), _pad_lanes(_swap_rot_halves(pe))])
    return jnp.pad(rows, ((0, 5), (0, 0)))


def _moe_layer(xf, aff, gate_f, wg, wu, wd, *, layer, tm):
    t = xf.shape[0]
    cap = EC_FACTOR * t // N_EXPERTS
    tok, gate = expert_choice(aff[:, :N_EXPERTS].T, cap=cap)
    return moe_apply(xf, tok, gate, gate_f, wg, wu, wd, layer=layer, tm=min(tm, cap))


def kernel(x, c, ctx, c_ctx, ada_w, ada_b, norm_mix, norm_ffn, mla_w_in, mla_q_a_gain, mla_kv_a_gain,
           mla_w_qb, mla_w_kvb, mla_q_gain, mla_k_gain, mla_w_o, na_w_qkv, na_q_gain, na_k_gain, na_rpb,
           na_w_o, router_w, moe_w_gate, moe_w_up, moe_w_down):
    n = x.shape[1]
    n_ctx = ctx.shape[1]
    d = D_MODEL
    x_lat = x[0]
    x_ctx = ctx[0]

    mods = ada_modulation(jnp.stack([c[0], c_ctx]), ada_w, ada_b)

    def mod(layer, r, k):
        return mods[layer, r:r + 1, k * d:(k + 1) * d]

    row = lambda v: v.reshape(1, -1)
    rw_hi = [_pad_lanes(router_w[i]).astype(BF16) for i in range(2)]
    rw_lo = [(_pad_lanes(router_w[i]) - rw_hi[i].astype(F32)).astype(BF16) for i in range(2)]
    wg = moe_w_gate.astype(BF16)
    wu = moe_w_up.astype(BF16)
    wd = moe_w_down.astype(BF16)

    w_in = mla_w_in[0]
    kpe_w = w_in[:, Q_LORA + KV_LORA:]
    w_in_ext = jnp.concatenate(
        [w_in[:, :Q_LORA + KV_LORA], _pad_lanes(kpe_w), _pad_lanes(_swap_rot_halves(kpe_w))], axis=1).astype(BF16)
    wq = mla_w_qb[0].reshape(Q_LORA, MLA_HEADS, QK_HEAD)
    wq_ext = jnp.concatenate(
        [wq[..., :QK_NOPE], _pad_lanes(wq[..., QK_NOPE:]), _pad_lanes(_swap_rot_halves(wq[..., QK_NOPE:]))],
        axis=-1).reshape(Q_LORA, -1).astype(BF16)
    wkv = mla_w_kvb[0].astype(BF16)
    gq = _gain_rows(mla_q_gain[0])
    gk = _gain_rows(mla_k_gain[0])
    rope_c, rope_s = _rope_tables(n)
    ones_c = _pad_lanes(jnp.ones((n_ctx, QK_ROPE), F32))
    zeros_s = jnp.zeros((n_ctx, LANES), F32)
    no_gain = jnp.ones((1, w_in_ext.shape[1]), F32)

    a_lat = norm_mod_proj(x_lat, row(norm_mix[0]), mod(0, 0, 1), mod(0, 0, 0), w_in_ext, no_gain,
                          tm=512, tn=w_in_ext.shape[1], out_dtype=F32)
    a_ctx = norm_mod_proj(x_ctx, row(norm_mix[0]), mod(0, 1, 1), mod(0, 1, 0), w_in_ext, no_gain,
                          tm=n_ctx, tn=w_in_ext.shape[1], out_dtype=F32)
    q_l, k_l, v_l = mla_qkv(a_lat, row(mla_q_a_gain[0]), row(mla_kv_a_gain[0]), wq_ext, wkv, gq, gk,
                            rope_c, rope_s, tm=256)
    q_c, k_c, v_c = mla_qkv(a_ctx, row(mla_q_a_gain[0]), row(mla_kv_a_gain[0]), wq_ext, wkv, gq, gk,
                            ones_c, zeros_s, tm=n_ctx)
    o_lat = mla_attention(q_l, k_c, v_c, k_l, v_l, tq=1024, tk=1024)
    o_ctx = mla_attention(q_c, k_c, v_c, tq=n_ctx)

    w_o = mla_w_o[0].astype(BF16)
    x_lat, aff_lat = outproj_ffnprep(o_lat, w_o, x_lat, mod(0, 0, 2), row(norm_ffn[0]), mod(0, 0, 4),
                                     mod(0, 0, 3), rw_hi[0], rw_lo[0], tm=512)
    x_ctx, aff_ctx = outproj_ffnprep(o_ctx, w_o, x_ctx, mod(0, 1, 2), row(norm_ffn[0]), mod(0, 1, 4),
                                     mod(0, 1, 3), rw_hi[0], rw_lo[0], tm=n_ctx)
    x_lat = _moe_layer(x_lat, aff_lat, mod(0, 0, 5), wg, wu, wd, layer=0, tm=256)
    x_ctx = _moe_layer(x_ctx, aff_ctx, mod(0, 1, 5), wg, wu, wd, layer=0, tm=256)

    w_qkv = na_w_qkv[0].astype(BF16)
    qkv_gain = jnp.concatenate([jnp.tile(na_q_gain[0], NA_HEADS) * (NA_HEAD_DIM ** -0.5 * LOG2_E),
                                jnp.tile(na_k_gain[0], NA_HEADS), jnp.ones((d,), F32)]).reshape(1, -1)
    qkv_l = norm_mod_proj(x_lat, row(norm_mix[1]), mod(1, 0, 1), mod(1, 0, 0), w_qkv, qkv_gain,
                          tm=1024, tn=1024, out_dtype=BF16, n_norm_blocks=4)
    qkv_c = norm_mod_proj(x_ctx, row(norm_mix[1]), mod(1, 1, 1), mod(1, 1, 0), w_qkv, qkv_gain,
                          tm=n_ctx, tn=1024, out_dtype=BF16, n_norm_blocks=4)
    o_lat = na_attention(qkv_l, qkv_c, na_rpb[0])
    x_lat, aff_lat = outproj_ffnprep(o_lat, na_w_o[0].astype(BF16), x_lat, mod(1, 0, 2), row(norm_ffn[1]),
                                     mod(1, 0, 4), mod(1, 0, 3), rw_hi[1], rw_lo[1], tm=512)
    x_lat = _moe_layer(x_lat, aff_lat, mod(1, 0, 5), wg, wu, wd, layer=1, tm=256)
    return x_lat[None, :, :d]
```

```python
import functools
import math

import jax
import jax.numpy as jnp
from jax import lax
from jax.experimental import pallas as pl
from jax.experimental.pallas import tpu as pltpu

F32 = jnp.float32
BF16 = jnp.bfloat16

D_MODEL = 2048
GRID_W = 64
EPS = 1e-6
ROPE_THETA = 10000.0

MLA_HEADS = 16
Q_LORA = 512
KV_LORA = 512
QK_NOPE = 128
QK_ROPE = 64
QK_HEAD = QK_NOPE + QK_ROPE
V_DIM = 128
MLA_QK_PAD = 256

NA_HEADS = 16
NA_HEAD_DIM = 128
NA_KH = 8
NA_KW = 16

N_EXPERTS = 16
EXPERT_FF = 1408
EC_FACTOR = 2

LANES = 128
SUBLANES = 8
NEG_BIG = -1e30
LOG2_E = math.log2(math.e)

VMEM_LIMIT = 56 * 1024 * 1024


def _cparams(sem):
    return pltpu.CompilerParams(dimension_semantics=sem, vmem_limit_bytes=VMEM_LIMIT)


def _ada_kernel(c_ref, w_ref, b_ref, o_ref, *, tn):
    for r in range(2):
        c = c_ref[r]
        act = c * (1.0 / (1.0 + jnp.exp(-c)))
        for k in range(tn // LANES):
            sl = slice(k * LANES, (k + 1) * LANES)
            o_ref[0, r:r + 1, sl] = jnp.sum(act * w_ref[0, :, sl], axis=0, keepdims=True) + b_ref[0, :, sl]


def ada_modulation(cvec, ada_w, ada_b, *, tn=512):
    depth, d, n = ada_w.shape
    c_rep = jnp.broadcast_to(cvec[:, :, None], (2, d, LANES))
    return pl.pallas_call(
        functools.partial(_ada_kernel, tn=tn),
        out_shape=jax.ShapeDtypeStruct((depth, 2, n), F32),
        grid=(depth, n // tn),
        in_specs=[
            pl.BlockSpec((2, d, LANES), lambda l, j: (0, 0, 0)),
            pl.BlockSpec((1, d, tn), lambda l, j: (l, 0, j)),
            pl.BlockSpec((1, 1, tn), lambda l, j: (l, 0, j)),
        ],
        out_specs=pl.BlockSpec((1, 2, tn), lambda l, j: (l, 0, j)),
        compiler_params=_cparams(("arbitrary", "arbitrary")),
        name="ada_modulation",
    )(c_rep, ada_w, ada_b.reshape(depth, 1, n))


def _proj_kernel(x_ref, g_ref, sc_ref, sh_ref, w_ref, cg_ref, o_ref, h_ref, *, tn, n_norm_blocks, head):
    j = pl.program_id(1)

    @pl.when(j == 0)
    def _():
        x = x_ref[...]
        y = x * lax.rsqrt(jnp.mean(x * x, axis=-1, keepdims=True) + EPS) * g_ref[...]
        h_ref[...] = (y * (1.0 + sc_ref[...]) + sh_ref[...]).astype(BF16)

    acc = jnp.dot(h_ref[...], w_ref[...], preferred_element_type=F32)
    if n_norm_blocks == 0:
        o_ref[...] = acc.astype(o_ref.dtype)
    else:
        @pl.when(j < n_norm_blocks)
        def _():
            for c in range(tn // head):
                sl = slice(c * head, (c + 1) * head)
                y = acc[:, sl]
                r = lax.rsqrt(jnp.mean(y * y, axis=-1, keepdims=True) + EPS)
                o_ref[:, sl] = (y * r * cg_ref[:, sl]).astype(o_ref.dtype)

        @pl.when(j >= n_norm_blocks)
        def _():
            o_ref[...] = acc.astype(o_ref.dtype)


def norm_mod_proj(x, gain, scale, shift, w, col_gain, *, tm, tn, out_dtype, n_norm_blocks=0, head=LANES):
    m = x.shape[0]
    d, n = w.shape
    return pl.pallas_call(
        functools.partial(_proj_kernel, tn=tn, n_norm_blocks=n_norm_blocks, head=head),
        out_shape=jax.ShapeDtypeStruct((m, n), out_dtype),
        grid=(m // tm, n // tn),
        in_specs=[
            pl.BlockSpec((tm, d), lambda i, j: (i, 0)),
            pl.BlockSpec((1, d), lambda i, j: (0, 0)),
            pl.BlockSpec((1, d), lambda i, j: (0, 0)),
            pl.BlockSpec((1, d), lambda i, j: (0, 0)),
            pl.BlockSpec((d, tn), lambda i, j: (0, j)),
            pl.BlockSpec((1, tn), lambda i, j: (0, j)),
        ],
        out_specs=pl.BlockSpec((tm, tn), lambda i, j: (i, j)),
        scratch_shapes=[pltpu.VMEM((tm, d), BF16)],
        compiler_params=_cparams(("arbitrary", "arbitrary")),
        name="norm_mod_proj",
    )(x, gain, scale, shift, w, col_gain)


def _mla_qkv_kernel(a_ref, qag_ref, kvag_ref, wq_ref, wkv_ref, gq_ref, gk_ref, rc_ref, rs_ref,
                    q_ref, k_ref, v_ref, *, heads, q_scale):
    a = a_ref[...]
    qc = a[:, :Q_LORA]
    kvc = a[:, Q_LORA:Q_LORA + KV_LORA]
    kpe1 = a[:, Q_LORA + KV_LORA:Q_LORA + KV_LORA + LANES]
    kpe2 = a[:, Q_LORA + KV_LORA + LANES:]

    def rms(x, g):
        return x * lax.rsqrt(jnp.mean(x * x, axis=-1, keepdims=True) + EPS) * g

    qn = rms(qc, qag_ref[...]).astype(BF16)
    kvn = rms(kvc, kvag_ref[...]).astype(BF16)
    rc = rc_ref[...]
    rs = rs_ref[...]
    gq_nope, gq1, gq2 = gq_ref[0:1, :], gq_ref[1:2, :], gq_ref[2:3, :]
    gk_nope, gk1, gk2 = gk_ref[0:1, :], gk_ref[1:2, :], gk_ref[2:3, :]
    kpe_ss = jnp.sum(kpe1 * kpe1, axis=-1, keepdims=True)
    kpe_rot = kpe1 * gk1 * rc + kpe2 * gk2 * rs
    qw = QK_NOPE + 2 * LANES
    for h in range(heads):
        qh = jnp.dot(qn, wq_ref[:, h * qw:(h + 1) * qw], preferred_element_type=F32)
        qa, q1, q2 = qh[:, :QK_NOPE], qh[:, QK_NOPE:QK_NOPE + LANES], qh[:, QK_NOPE + LANES:]
        ss = jnp.sum(qa * qa, axis=-1, keepdims=True) + jnp.sum(q1 * q1, axis=-1, keepdims=True)
        r = lax.rsqrt(ss * (1.0 / QK_HEAD) + EPS) * q_scale
        q_ref[:, h * MLA_QK_PAD:h * MLA_QK_PAD + QK_NOPE] = (qa * r * gq_nope).astype(BF16)
        q_ref[:, h * MLA_QK_PAD + QK_NOPE:(h + 1) * MLA_QK_PAD] = (
            (q1 * gq1 * rc + q2 * gq2 * rs) * r).astype(BF16)
        kvh = jnp.dot(kvn, wkv_ref[:, h * (QK_NOPE + V_DIM):(h + 1) * (QK_NOPE + V_DIM)],
                      preferred_element_type=F32)
        kn, vv = kvh[:, :QK_NOPE], kvh[:, QK_NOPE:]
        rk = lax.rsqrt((jnp.sum(kn * kn, axis=-1, keepdims=True) + kpe_ss) * (1.0 / QK_HEAD) + EPS)
        k_ref[:, h * MLA_QK_PAD:h * MLA_QK_PAD + QK_NOPE] = (kn * rk * gk_nope).astype(BF16)
        k_ref[:, h * MLA_QK_PAD + QK_NOPE:(h + 1) * MLA_QK_PAD] = (kpe_rot * rk).astype(BF16)
        v_ref[:, h * V_DIM:(h + 1) * V_DIM] = vv.astype(BF16)


def mla_qkv(a, q_a_gain, kv_a_gain, wq_ext, wkv, gq, gk, rope_c, rope_s, *, tm, heads=MLA_HEADS):
    m, aw = a.shape
    full = lambda i: (0, 0)
    return pl.pallas_call(
        functools.partial(_mla_qkv_kernel, heads=heads, q_scale=QK_HEAD ** -0.5 * LOG2_E),
        out_shape=(
            jax.ShapeDtypeStruct((m, heads * MLA_QK_PAD), BF16),
            jax.ShapeDtypeStruct((m, heads * MLA_QK_PAD), BF16),
            jax.ShapeDtypeStruct((m, heads * V_DIM), BF16),
        ),
        grid=(m // tm,),
        in_specs=[
            pl.BlockSpec((tm, aw), lambda i: (i, 0)),
            pl.BlockSpec((1, Q_LORA), full),
            pl.BlockSpec((1, KV_LORA), full),
            pl.BlockSpec(wq_ext.shape, full),
            pl.BlockSpec(wkv.shape, full),
            pl.BlockSpec((8, LANES), full),
            pl.BlockSpec((8, LANES), full),
            pl.BlockSpec((tm, LANES), lambda i: (i, 0)),
            pl.BlockSpec((tm, LANES), lambda i: (i, 0)),
        ],
        out_specs=(
            pl.BlockSpec((tm, heads * MLA_QK_PAD), lambda i: (i, 0)),
            pl.BlockSpec((tm, heads * MLA_QK_PAD), lambda i: (i, 0)),
            pl.BlockSpec((tm, heads * V_DIM), lambda i: (i, 0)),
        ),
        compiler_params=_cparams(("arbitrary",)),
        name="mla_qkv",
    )(a, q_a_gain, kv_a_gain, wq_ext, wkv, gq, gk, rope_c, rope_s)


def _nt_dot(a, b):
    return lax.dot_general(a, b, (((1,), (1,)), ((), ())), preferred_element_type=F32)


ATTN_UNROLL = 8


def _mla_attn_kernel(*refs, tk, n_lat):
    if n_lat:
        q_ref, kc_ref, vc_ref, kl_ref, vl_ref, o_ref = refs
    else:
        q_ref, kc_ref, vc_ref, o_ref = refs
    tq = q_ref.shape[0]

    def step(k, v, m, l, acc):
        s = _nt_dot(q_ref[...], k)
        m_new = jnp.maximum(m, jnp.max(s, axis=-1, keepdims=True))
        p = jnp.exp2(s - m_new)
        alpha = jnp.exp2(m - m_new)
        l_new = alpha * l + jnp.sum(p, axis=-1, keepdims=True)
        acc_new = alpha * acc + jnp.dot(p.astype(BF16), v, preferred_element_type=F32)
        return m_new, l_new, acc_new

    m0 = jnp.full((tq, 1), NEG_BIG, F32)
    l0 = jnp.zeros((tq, 1), F32)
    acc0 = jnp.zeros((tq, V_DIM), F32)
    carry = step(kc_ref[...], vc_ref[...], m0, l0, acc0)
    if n_lat:
        def body(i, carry):
            start = pl.multiple_of(i * tk, tk)
            return step(kl_ref[pl.ds(start, tk), :], vl_ref[pl.ds(start, tk), :], *carry)

        carry = lax.fori_loop(0, n_lat // tk, body, carry, unroll=ATTN_UNROLL)
    _, l, acc = carry
    o_ref[...] = (acc * (1.0 / l)).astype(o_ref.dtype)


def mla_attention(q, k_ctx, v_ctx, k_lat=None, v_lat=None, *, tq, tk=512, heads=MLA_HEADS):
    nq = q.shape[0]
    n_ctx = k_ctx.shape[0]
    n_lat = 0 if k_lat is None else k_lat.shape[0]
    in_specs = [
        pl.BlockSpec((tq, MLA_QK_PAD), lambda h, i: (i, h)),
        pl.BlockSpec((n_ctx, MLA_QK_PAD), lambda h, i: (0, h)),
        pl.BlockSpec((n_ctx, V_DIM), lambda h, i: (0, h)),
    ]
    args = [q, k_ctx, v_ctx]
    if n_lat:
        in_specs += [
            pl.BlockSpec((n_lat, MLA_QK_PAD), lambda h, i: (0, h)),
            pl.BlockSpec((n_lat, V_DIM), lambda h, i: (0, h)),
        ]
        args += [k_lat, v_lat]
    return pl.pallas_call(
        functools.partial(_mla_attn_kernel, tk=tk, n_lat=n_lat),
        out_shape=jax.ShapeDtypeStruct((nq, heads * V_DIM), BF16),
        grid=(heads, nq // tq),
        in_specs=in_specs,
        out_specs=pl.BlockSpec((tq, V_DIM), lambda h, i: (i, h)),
        compiler_params=_cparams(("arbitrary", "arbitrary")),
        name="mla_attention",
    )(*args)


def _na_build_bias(rpb_ref, bias_sc, *, rb, span_rows, rows):
    nblk = rows // rb
    qc = lax.broadcasted_iota(jnp.int32, (GRID_W, LANES), 0)
    lane = lax.broadcasted_iota(jnp.int32, (GRID_W, LANES), 1)
    kc = lane & (GRID_W - 1)
    col_start = jnp.clip(qc - NA_KW // 2, 0, GRID_W - NA_KW)
    col_ok = (kc >= col_start) & (kc < col_start + NA_KW)
    first = lane < GRID_W
    neg = jnp.full((GRID_W, LANES), NEG_BIG, F32)

    def toeplitz(dr, shift):
        w = jnp.broadcast_to(rpb_ref[0, dr + NA_KH - 1:dr + NA_KH, :] * LOG2_E, (GRID_W, LANES))
        return pltpu.roll(w, shift, 1, stride=1, stride_axis=0)

    pairs = {}

    def pair(dr, ok0, ok1):
        key = (dr, ok0, ok1)
        if key not in pairs:
            a = toeplitz(dr, LANES - (NA_KW - 1)) if ok0 else neg
            b = toeplitz(dr + 1, GRID_W - (NA_KW - 1)) if ok1 else neg
            pairs[key] = jnp.where(col_ok, jnp.where(first, a, b), NEG_BIG)
        return pairs[key]

    for v, blk in enumerate((0, 1, nblk - 1)):
        kr0 = min(max(blk * rb - NA_KH // 2, 0), rows - span_rows)
        for qr in range(rb):
            r = blk * rb + qr
            r0 = min(max(r - NA_KH // 2, 0), rows - NA_KH)
            for m in range(span_rows // 2):
                ka = kr0 + 2 * m
                ok0 = r0 <= ka < r0 + NA_KH
                ok1 = r0 <= ka + 1 < r0 + NA_KH
                blk_val = pair(ka - r, ok0, ok1) if (ok0 or ok1) else neg
                bias_sc[v, qr * GRID_W:(qr + 1) * GRID_W, m * LANES:(m + 1) * LANES] = blk_val


def _na_attn_kernel(q_ref, k_ref, v_ref, kc_ref, vc_ref, rpb_ref, o_ref, bias_sc, *, rb, span_rows, rows, sub):
    nblk = rows // rb
    tq = rb * GRID_W

    @pl.when(pl.program_id(1) == 0)
    def _():
        _na_build_bias(rpb_ref, bias_sc, rb=rb, span_rows=span_rows, rows=rows)

    span = span_rows * GRID_W
    for u in range(sub):
        i = pl.program_id(1) * sub + u
        kr0 = jnp.clip(i * rb - NA_KH // 2, 0, rows - span_rows)
        start = pl.multiple_of(kr0 * GRID_W, (NA_KH // 2) * GRID_W)
        variant = jnp.where(i == 0, 0, jnp.where(i == nblk - 1, 2, 1))
        q = q_ref[u * tq:(u + 1) * tq, :]
        kw = k_ref[pl.ds(start, span), :]
        vw = v_ref[pl.ds(start, span), :]
        s_win = _nt_dot(q, kw) + bias_sc[variant]
        s_ctx = _nt_dot(q, kc_ref[...])
        m = jnp.maximum(jnp.max(s_win, axis=-1, keepdims=True), jnp.max(s_ctx, axis=-1, keepdims=True))
        p_win = jnp.exp2(s_win - m)
        p_ctx = jnp.exp2(s_ctx - m)
        l = jnp.sum(p_win, axis=-1, keepdims=True) + jnp.sum(p_ctx, axis=-1, keepdims=True)
        o = (jnp.dot(p_win.astype(BF16), vw, preferred_element_type=F32)
             + jnp.dot(p_ctx.astype(BF16), vc_ref[...], preferred_element_type=F32))
        o_ref[u * tq:(u + 1) * tq, :] = (o * (1.0 / l)).astype(o_ref.dtype)


def na_attention(qkv, qkv_ctx, rpb, *, rb=8, sub=8, heads=NA_HEADS):
    n = qkv.shape[0]
    n_ctx = qkv_ctx.shape[0]
    rows = n // GRID_W
    span_rows = rb + NA_KH
    nblk = rows // rb
    sub = math.gcd(sub, nblk)
    tq = rb * GRID_W
    rpb_pad = jnp.pad(rpb, ((0, 0), (0, 2 * NA_KH - rpb.shape[1]), (0, LANES - rpb.shape[2])))
    return pl.pallas_call(
        functools.partial(_na_attn_kernel, rb=rb, span_rows=span_rows, rows=rows, sub=sub),
        out_shape=jax.ShapeDtypeStruct((n, heads * NA_HEAD_DIM), BF16),
        grid=(heads, nblk // sub),
        in_specs=[
            pl.BlockSpec((sub * tq, NA_HEAD_DIM), lambda h, i: (i, h)),
            pl.BlockSpec((n, NA_HEAD_DIM), lambda h, i: (0, heads + h)),
            pl.BlockSpec((n, NA_HEAD_DIM), lambda h, i: (0, 2 * heads + h)),
            pl.BlockSpec((n_ctx, NA_HEAD_DIM), lambda h, i: (0, heads + h)),
            pl.BlockSpec((n_ctx, NA_HEAD_DIM), lambda h, i: (0, 2 * heads + h)),
            pl.BlockSpec((1, 2 * NA_KH, LANES), lambda h, i: (h, 0, 0)),
        ],
        out_specs=pl.BlockSpec((sub * tq, NA_HEAD_DIM), lambda h, i: (i, h)),
        scratch_shapes=[pltpu.VMEM((3, tq, span_rows * GRID_W), F32)],
        compiler_params=_cparams(("arbitrary", "arbitrary")),
        name="na_attention",
    )(qkv, qkv, qkv, qkv_ctx, qkv_ctx, rpb_pad)


def _outproj_kernel(o_ref, w_ref, x_ref, ga_ref, g_ref, sc_ref, sh_ref, rwh_ref, rwl_ref,
                    xf_ref, aff_ref, *, n_experts):
    d = x_ref.shape[1]
    acc = jnp.dot(o_ref[...], w_ref[...], preferred_element_type=F32)
    xn = x_ref[...] + ga_ref[...] * acc
    xf_ref[:, :d] = xn
    y = xn * lax.rsqrt(jnp.mean(xn * xn, axis=-1, keepdims=True) + EPS) * g_ref[...]
    f = y * (1.0 + sc_ref[...]) + sh_ref[...]
    xf_ref[:, d:] = f
    f_hi = f.astype(BF16)
    f_lo = (f - f_hi.astype(F32)).astype(BF16)
    logits = (jnp.dot(f_hi, rwh_ref[...], preferred_element_type=F32)
              + jnp.dot(f_lo, rwh_ref[...], preferred_element_type=F32)
              + jnp.dot(f_hi, rwl_ref[...], preferred_element_type=F32))
    lane = lax.broadcasted_iota(jnp.int32, logits.shape, 1)
    logits = jnp.where(lane < n_experts, logits, NEG_BIG)
    e = jnp.exp(logits - jnp.max(logits, axis=-1, keepdims=True))
    aff_ref[...] = e * (1.0 / jnp.sum(e, axis=-1, keepdims=True))


def outproj_ffnprep(o, w_o, x, gate_a, gain, scale, shift, rw_hi, rw_lo, *, tm, n_experts=N_EXPERTS):
    m = x.shape[0]
    k, d = w_o.shape
    row = lambda i: (i, 0)
    full = lambda i: (0, 0)
    return pl.pallas_call(
        functools.partial(_outproj_kernel, n_experts=n_experts),
        out_shape=(
            jax.ShapeDtypeStruct((m, 2 * d), F32),
            jax.ShapeDtypeStruct((m, LANES), F32),
        ),
        grid=(m // tm,),
        in_specs=[
            pl.BlockSpec((tm, k), row),
            pl.BlockSpec((k, d), full),
            pl.BlockSpec((tm, d), row),
            pl.BlockSpec((1, d), full),
            pl.BlockSpec((1, d), full),
            pl.BlockSpec((1, d), full),
            pl.BlockSpec((1, d), full),
            pl.BlockSpec((d, LANES), full),
            pl.BlockSpec((d, LANES), full),
        ],
        out_specs=(
            pl.BlockSpec((tm, 2 * d), row),
            pl.BlockSpec((tm, LANES), row),
        ),
        compiler_params=_cparams(("arbitrary",)),
        name="outproj_ffnprep",
    )(o, w_o, x, gate_a, gain, scale, shift, rw_hi, rw_lo)


def _prefix_rank(sel_bf, upper_incl, lower_strict):
    w = jnp.dot(sel_bf, upper_incl, preferred_element_type=F32)
    row_tot = jnp.broadcast_to(w[:, LANES - 1:LANES], w.shape)
    row_off = jnp.dot(lower_strict, row_tot.astype(BF16), preferred_element_type=F32)
    return w, row_tot, row_off


def _topk_kernel(a_ref, tok_ref, gate_ref, *, cap, jc):
    a = a_ref[0]
    tb = a.shape[0]
    ri = lax.broadcasted_iota(jnp.int32, (tb, LANES), 0)
    ci = lax.broadcasted_iota(jnp.int32, (tb, LANES), 1)
    upper_incl = (ri <= ci).astype(BF16)
    lower_strict = (ci < ri).astype(BF16)

    def count(mask):
        c = jnp.sum(mask.astype(F32), axis=1, keepdims=True)
        return jnp.sum(c, axis=0, keepdims=True)

    thr = jnp.zeros((1, 1), jnp.int32)
    for bit in range(30, -1, -1):
        cand = thr | jnp.int32(1 << bit)
        candf = lax.bitcast_convert_type(cand, F32)
        thr = jnp.where(count(a >= candf) >= cap, cand, thr)
    thrf = lax.bitcast_convert_type(thr, F32)
    gt = a > thrf
    eq = a == thrf
    need = cap - count(gt)
    w_eq, _, off_eq = _prefix_rank(eq.astype(BF16), upper_incl, lower_strict)
    sel = gt | (eq & (w_eq + off_eq <= need))
    sel_bf = sel.astype(BF16)
    w, row_tot, row_off = _prefix_rank(sel_bf, upper_incl, lower_strict)
    tot_t = row_tot.T
    incl_t = jnp.dot(tot_t.astype(BF16), upper_incl, preferred_element_type=F32)
    excl_t = incl_t - tot_t
    excl_row = excl_t[0:1, :]
    incl_row = incl_t[0:1, :]
    w_bf = w.astype(BF16)
    a1 = a.astype(BF16)
    a2 = (a - a1.astype(F32)).astype(BF16)
    a3 = (a - a1.astype(F32) - a2.astype(F32)).astype(BF16)
    lane_f = lax.broadcasted_iota(jnp.int32, (jc, LANES), 1).astype(F32)
    for c in range(cap // jc):
        slot = (lax.broadcasted_iota(jnp.int32, (jc, 1), 0) + c * jc).astype(F32)
        in_row = (excl_row <= slot) & (slot < incl_row)
        in_row_f = in_row.astype(F32)
        in_row_bf = in_row.astype(BF16)
        w_g = jnp.dot(in_row_bf, w_bf, preferred_element_type=F32)
        s_g = jnp.dot(in_row_bf, sel_bf, preferred_element_type=F32)
        off_g = jnp.sum(in_row_f * excl_row, axis=1, keepdims=True)
        hit = (s_g > 0.5) & (w_g + off_g == slot + 1.0)
        hit_f = hit.astype(F32)
        row_id = jnp.sum(in_row_f * lane_f, axis=1, keepdims=True)
        lane_id = jnp.sum(hit_f * lane_f, axis=1, keepdims=True)
        a_g = (jnp.dot(in_row_bf, a1, preferred_element_type=F32)
               + jnp.dot(in_row_bf, a2, preferred_element_type=F32)
               + jnp.dot(in_row_bf, a3, preferred_element_type=F32))
        tok_ref[0, c * jc:(c + 1) * jc, :] = (row_id * float(LANES) + lane_id).astype(jnp.int32)
        gate_ref[0, c * jc:(c + 1) * jc, :] = jnp.sum(hit_f * a_g, axis=1, keepdims=True)


def expert_choice(aff_t, *, cap):
    e, t = aff_t.shape
    t_pad = LANES * LANES
    a = jnp.pad(aff_t, ((0, 0), (0, t_pad - t)), constant_values=-1.0).reshape(e, LANES, LANES)
    jc = min(cap, 256)
    tok, gate = pl.pallas_call(
        functools.partial(_topk_kernel, cap=cap, jc=jc),
        out_shape=(
            jax.ShapeDtypeStruct((e, cap, 1), jnp.int32),
            jax.ShapeDtypeStruct((e, cap, 1), F32),
        ),
        grid=(e,),
        in_specs=[pl.BlockSpec((1, LANES, LANES), lambda i: (i, 0, 0))],
        out_specs=(
            pl.BlockSpec((1, cap, 1), lambda i: (i, 0, 0)),
            pl.BlockSpec((1, cap, 1), lambda i: (i, 0, 0)),
        ),
        compiler_params=_cparams(("arbitrary",)),
        name="expert_choice",
    )(a)
    return tok.reshape(e, cap), gate


def _moe_kernel(idx_hbm, xf_hbm, g_ref, gf_ref, wg_ref, wu_ref, wd_ref, o_hbm,
                idx_s, xs, sem_idx, sem_f, sem_x, sem_o, *, tm, chunks, n_steps):
    del xf_hbm
    step = pl.program_id(0) * chunks + pl.program_id(1)
    stride = idx_hbm.shape[1]
    cur, nxt, prv = step % 3, (step + 1) % 3, (step + 2) % 3
    d = xs.shape[-1] // 2

    def idx_copy(s, slot):
        return pltpu.make_async_copy(
            idx_hbm.at[s], idx_s.at[pl.ds(pl.multiple_of(slot * stride, stride), stride)], sem_idx)

    def row_in(g, u, slot):
        tok = idx_s[slot * stride + g * SUBLANES + u]
        return pltpu.make_async_copy(o_hbm.at[pl.ds(tok, 1)], xs.at[slot, g, pl.ds(u, 1)], sem_f.at[slot])

    def f_in(g, u, slot):
        tok = idx_s[slot * stride + g * SUBLANES + u]
        return pltpu.make_async_copy(o_hbm.at[pl.ds(tok, 1), pl.ds(d, d)],
                                     xs.at[slot, g, pl.ds(u, 1), pl.ds(d, d)], sem_f.at[slot])

    def x_refresh(g, u):
        tok = idx_s[cur * stride + g * SUBLANES + u]
        return pltpu.make_async_copy(o_hbm.at[pl.ds(tok, 1), pl.ds(0, d)],
                                     xs.at[cur, g, pl.ds(u, 1), pl.ds(0, d)], sem_x)

    def x_row_out(g, u, slot):
        tok = idx_s[slot * stride + g * SUBLANES + u]
        return pltpu.make_async_copy(xs.at[slot, g, pl.ds(u, 1), pl.ds(0, d)],
                                     o_hbm.at[pl.ds(tok, 1), pl.ds(0, d)], sem_o)

    def rows(fn):
        def body(g, c):
            for u in range(SUBLANES):
                fn(g, u)
            return c
        lax.fori_loop(0, tm // SUBLANES, body, 0)

    @pl.when(step == 0)
    def _():
        idx_copy(0, 0).start()
        idx_copy(0, 0).wait()
        rows(lambda g, u: row_in(g, u, 0).start(priority=u % 2))

    has_next = step + 1 < n_steps
    last_chunk = pl.program_id(1) == chunks - 1
    new_expert = (step > 0) & (pl.program_id(1) == 0)
    same_expert = (step > 0) & (pl.program_id(1) != 0)

    @pl.when(has_next)
    def _():
        idx_copy(step + 1, nxt).start()
        idx_copy(step + 1, nxt).wait()

    @pl.when(has_next & last_chunk)
    def _():
        rows(lambda g, u: f_in(g, u, nxt).start(priority=u % 2))

    @pl.when(has_next & jnp.logical_not(last_chunk))
    def _():
        rows(lambda g, u: row_in(g, u, nxt).start(priority=u % 2))

    @pl.when(new_expert)
    def _():
        rows(lambda g, u: x_row_out(g, u, prv).wait())
        rows(lambda g, u: x_refresh(g, u).start(priority=u % 2))
        rows(lambda g, u: f_in(g, u, cur).wait())

    @pl.when(jnp.logical_not(new_expert))
    def _():
        rows(lambda g, u: row_in(g, u, cur).wait())

    xb = xs[cur, :, :, d:].reshape(tm, d).astype(BF16)
    hg = jnp.dot(xb, wg_ref[0, 0], preferred_element_type=F32)
    hu = jnp.dot(xb, wu_ref[0, 0], preferred_element_type=F32)
    hid = (hg * (1.0 / (1.0 + jnp.exp(-hg)))) * hu
    y = jnp.dot(hid.astype(BF16), wd_ref[0, 0], preferred_element_type=F32)
    upd = gf_ref[...] * (y * g_ref[...])

    @pl.when(new_expert)
    def _():
        rows(lambda g, u: x_refresh(g, u).wait())

    xs[cur, :, :, :d] = xs[cur, :, :, :d] + upd.reshape(tm // SUBLANES, SUBLANES, d)

    @pl.when(same_expert)
    def _():
        rows(lambda g, u: x_row_out(g, u, prv).wait())

    rows(lambda g, u: x_row_out(g, u, cur).start(priority=u % 2))

    @pl.when(step == n_steps - 1)
    def _():
        rows(lambda g, u: x_row_out(g, u, cur).wait())


def moe_apply(xf, tok, gate, gate_f, wg, wu, wd, *, layer, tm):
    t = xf.shape[0]
    d = xf.shape[1] // 2
    e, cap = tok.shape
    chunks = cap // tm
    ff = wg.shape[3]
    stride = max(tm, LANES)
    tok_rows = jnp.pad(tok.reshape(e * chunks, tm), ((0, 0), (0, stride - tm)))
    return pl.pallas_call(
        functools.partial(_moe_kernel, tm=tm, chunks=chunks, n_steps=e * chunks),
        out_shape=jax.ShapeDtypeStruct((t, 2 * d), F32),
        grid=(e, chunks),
        in_specs=[
            pl.BlockSpec(memory_space=pl.ANY),
            pl.BlockSpec(memory_space=pl.ANY),
            pl.BlockSpec((tm, 1), lambda i, c: (i * chunks + c, 0)),
            pl.BlockSpec((1, d), lambda i, c: (0, 0)),
            pl.BlockSpec((1, 1, d, ff), lambda i, c: (layer, i, 0, 0)),
            pl.BlockSpec((1, 1, d, ff), lambda i, c: (layer, i, 0, 0)),
            pl.BlockSpec((1, 1, ff, d), lambda i, c: (layer, i, 0, 0)),
        ],
        out_specs=pl.BlockSpec(memory_space=pl.ANY),
        scratch_shapes=[
            pltpu.SMEM((3 * stride,), jnp.int32),
            pltpu.VMEM((3, tm // SUBLANES, SUBLANES, 2 * d), F32),
            pltpu.SemaphoreType.DMA,
            pltpu.SemaphoreType.DMA((3,)),
            pltpu.SemaphoreType.DMA,
            pltpu.SemaphoreType.DMA,
        ],
        input_output_aliases={1: 0},
        compiler_params=_cparams(("arbitrary", "arbitrary")),
        name="moe_apply",
    )(tok_rows, xf, gate.reshape(e * cap, 1), gate_f, wg, wu, wd)


def _swap_rot_halves(w):
    q = QK_ROPE // 4
    return jnp.concatenate([w[..., q:2 * q], w[..., :q], w[..., 3 * q:], w[..., 2 * q:3 * q]], axis=-1)


def _pad_lanes(w):
    return jnp.pad(w, [(0, 0)] * (w.ndim - 1) + [(0, LANES - w.shape[-1])])


def _rope_tables(n):
    half = QK_ROPE // 2
    inv = ROPE_THETA ** (-jnp.arange(0, half, 2, dtype=F32) / half)
    t = jnp.arange(n)
    ang_r = (t // GRID_W).astype(F32)[:, None] * inv[None, :]
    ang_c = (t % GRID_W).astype(F32)[:, None] * inv[None, :]
    cr, sr, cc, sc = jnp.cos(ang_r), jnp.sin(ang_r), jnp.cos(ang_c), jnp.sin(ang_c)
    return (_pad_lanes(jnp.concatenate([cr, cr, cc, cc], axis=-1)),
            _pad_lanes(jnp.concatenate([-sr, sr, -sc, sc], axis=-1)))


def _gain_rows(g):
    pe = g[QK_NOPE:]
    rows = jnp.stack([g[:QK_NOPE], _pad_lanes(pe), _pad_lanes(_swap_rot_halves(pe))])
    return jnp.pad(rows, ((0, 5), (0, 0)))


def _moe_layer(xf, aff, gate_f, wg, wu, wd, *, layer, tm):
    t = xf.shape[0]
    cap = EC_FACTOR * t // N_EXPERTS
    tok, gate = expert_choice(aff[:, :N_EXPERTS].T, cap=cap)
    return moe_apply(xf, tok, gate, gate_f, wg, wu, wd, layer=layer, tm=min(tm, cap))


def kernel(x, c, ctx, c_ctx, ada_w, ada_b, norm_mix, norm_ffn, mla_w_in, mla_q_a_gain, mla_kv_a_gain,
           mla_w_qb, mla_w_kvb, mla_q_gain, mla_k_gain, mla_w_o, na_w_qkv, na_q_gain, na_k_gain, na_rpb,
           na_w_o, router_w, moe_w_gate, moe_w_up, moe_w_down):
    n = x.shape[1]
    n_ctx = ctx.shape[1]
    d = D_MODEL
    x_lat = x[0]
    x_ctx = ctx[0]

    mods = ada_modulation(jnp.stack([c[0], c_ctx]), ada_w, ada_b)

    def mod(layer, r, k):
        return mods[layer, r:r + 1, k * d:(k + 1) * d]

    row = lambda v: v.reshape(1, -1)
    rw_hi = [_pad_lanes(router_w[i]).astype(BF16) for i in range(2)]
    rw_lo = [(_pad_lanes(router_w[i]) - rw_hi[i].astype(F32)).astype(BF16) for i in range(2)]
    wg = moe_w_gate.astype(BF16)
    wu = moe_w_up.astype(BF16)
    wd = moe_w_down.astype(BF16)

    w_in = mla_w_in[0]
    kpe_w = w_in[:, Q_LORA + KV_LORA:]
    w_in_ext = jnp.concatenate(
        [w_in[:, :Q_LORA + KV_LORA], _pad_lanes(kpe_w), _pad_lanes(_swap_rot_halves(kpe_w))], axis=1).astype(BF16)
    wq = mla_w_qb[0].reshape(Q_LORA, MLA_HEADS, QK_HEAD)
    wq_ext = jnp.concatenate(
        [wq[..., :QK_NOPE], _pad_lanes(wq[..., QK_NOPE:]), _pad_lanes(_swap_rot_halves(wq[..., QK_NOPE:]))],
        axis=-1).reshape(Q_LORA, -1).astype(BF16)
    wkv = mla_w_kvb[0].astype(BF16)
    gq = _gain_rows(mla_q_gain[0])
    gk = _gain_rows(mla_k_gain[0])
    rope_c, rope_s = _rope_tables(n)
    ones_c = _pad_lanes(jnp.ones((n_ctx, QK_ROPE), F32))
    zeros_s = jnp.zeros((n_ctx, LANES), F32)
    no_gain = jnp.ones((1, w_in_ext.shape[1]), F32)

    a_lat = norm_mod_proj(x_lat, row(norm_mix[0]), mod(0, 0, 1), mod(0, 0, 0), w_in_ext, no_gain,
                          tm=512, tn=w_in_ext.shape[1], out_dtype=F32)
    a_ctx = norm_mod_proj(x_ctx, row(norm_mix[0]), mod(0, 1, 1), mod(0, 1, 0), w_in_ext, no_gain,
                          tm=n_ctx, tn=w_in_ext.shape[1], out_dtype=F32)
    q_l, k_l, v_l = mla_qkv(a_lat, row(mla_q_a_gain[0]), row(mla_kv_a_gain[0]), wq_ext, wkv, gq, gk,
                            rope_c, rope_s, tm=256)
    q_c, k_c, v_c = mla_qkv(a_ctx, row(mla_q_a_gain[0]), row(mla_kv_a_gain[0]), wq_ext, wkv, gq, gk,
                            ones_c, zeros_s, tm=n_ctx)
    o_lat = mla_attention(q_l, k_c, v_c, k_l, v_l, tq=1024, tk=1024)
    o_ctx = mla_attention(q_c, k_c, v_c, tq=n_ctx)

    w_o = mla_w_o[0].astype(BF16)
    x_lat, aff_lat = outproj_ffnprep(o_lat, w_o, x_lat, mod(0, 0, 2), row(norm_ffn[0]), mod(0, 0, 4),
                                     mod(0, 0, 3), rw_hi[0], rw_lo[0], tm=512)
    x_ctx, aff_ctx = outproj_ffnprep(o_ctx, w_o, x_ctx, mod(0, 1, 2), row(norm_ffn[0]), mod(0, 1, 4),
                                     mod(0, 1, 3), rw_hi[0], rw_lo[0], tm=n_ctx)
    x_lat = _moe_layer(x_lat, aff_lat, mod(0, 0, 5), wg, wu, wd, layer=0, tm=256)
    x_ctx = _moe_layer(x_ctx, aff_ctx, mod(0, 1, 5), wg, wu, wd, layer=0, tm=256)

    w_qkv = na_w_qkv[0].astype(BF16)
    qkv_gain = jnp.concatenate([jnp.tile(na_q_gain[0], NA_HEADS) * (NA_HEAD_DIM ** -0.5 * LOG2_E),
                                jnp.tile(na_k_gain[0], NA_HEADS), jnp.ones((d,), F32)]).reshape(1, -1)
    qkv_l = norm_mod_proj(x_lat, row(norm_mix[1]), mod(1, 0, 1), mod(1, 0, 0), w_qkv, qkv_gain,
                          tm=1024, tn=1024, out_dtype=BF16, n_norm_blocks=4)
    qkv_c = norm_mod_proj(x_ctx, row(norm_mix[1]), mod(1, 1, 1), mod(1, 1, 0), w_qkv, qkv_gain,
                          tm=n_ctx, tn=1024, out_dtype=BF16, n_norm_blocks=4)
    o_lat = na_attention(qkv_l, qkv_c, na_rpb[0])
    x_lat, aff_lat = outproj_ffnprep(o_lat, na_w_o[0].astype(BF16), x_lat, mod(1, 0, 2), row(norm_ffn[1]),
                                     mod(1, 0, 4), mod(1, 0, 3), rw_hi[1], rw_lo[1], tm=512)
    x_lat = _moe_layer(x_lat, aff_lat, mod(1, 0, 5), wg, wu, wd, layer=1, tm=256)
    return x_lat[None, :, :d]
```
